```python
import math
import jax, jax.numpy as jnp
from jax import lax
import numpy as np

D_MODEL = 2048
BATCH = 2
SEQ = 8192
DEPTH = 1

ATT_HEADS = 8
HEAD_DIM = 128
ATT_WIDTH = ATT_HEADS * HEAD_DIM
CONV_CHANNELS = D_MODEL - ATT_WIDTH
MIX_WIDTH = ATT_WIDTH + CONV_CHANNELS
IN_COLS = 3 * ATT_WIDTH + 2 * CONV_CHANNELS
DW_CONV_SIZE = 31
DILATED_PAIRS = ((128, 1), (512, 4), (2048, 16))
N_BUCKETS = 32
MAX_DISTANCE = 1024
N_EXPERTS = 16
EC_CAPACITY = 2
D_FF_EXPERT = 2 * D_MODEL
PLE_DIM = 256
EPS = 1e-6
NEG = -1e30

kernel_name = "hybrid_conv_dilated_attn_ec_moe_layer"


def rms_norm(x, g):
    xf = x.astype(jnp.float32)
    y = xf * lax.rsqrt(jnp.mean(xf * xf, axis=-1, keepdims=True) + EPS)
    return (y * g.astype(jnp.float32)).astype(x.dtype)


def layer_norm(x, g, b):
    xf = x.astype(jnp.float32)
    mu = jnp.mean(xf, axis=-1, keepdims=True)
    var = jnp.mean(jnp.square(xf - mu), axis=-1, keepdims=True)
    y = (xf - mu) * lax.rsqrt(var + EPS)
    return (y * g.astype(jnp.float32) + b.astype(jnp.float32)).astype(x.dtype)


def t5_bucket(rel):
    half_b = N_BUCKETS // 2
    exact = half_b // 2
    n = jnp.abs(rel)
    nf = jnp.maximum(n, 1).astype(jnp.float32)
    large = exact + (jnp.log(nf / exact) / math.log(MAX_DISTANCE / exact) * (half_b - exact)).astype(jnp.int32)
    large = jnp.minimum(large, half_b - 1)
    return jnp.where(rel > 0, half_b, 0) + jnp.where(n < exact, n, large)


def dilated_window_attention(q, k, v, rel_bias, window, dilation):
    B, S, H, Dh = q.shape
    half = window // (2 * dilation)
    blk = half
    L = S // dilation
    nb = -(-L // blk)
    Lp = nb * blk
    Z = B * dilation

    def to_sub(t):
        t = t.reshape(B, L, dilation, H, Dh).transpose(0, 2, 3, 1, 4)
        return t.reshape(Z, H, L, Dh)

    def band(t):
        t = jnp.pad(t, ((0, 0), (0, 0), (blk, Lp - L + blk), (0, 0))).reshape(Z, H, nb + 2, blk, Dh)
        return jnp.concatenate([t[:, :, :-2], t[:, :, 1:-1], t[:, :, 2:]], axis=3)

    qs = jnp.pad(to_sub(q), ((0, 0), (0, 0), (0, Lp - L), (0, 0))).reshape(Z, H, nb, blk, Dh)
    kb = band(to_sub(k))
    vb = band(to_sub(v))

    t_idx = jnp.arange(blk, dtype=jnp.int32)[:, None]
    u_idx = jnp.arange(3 * blk, dtype=jnp.int32)[None, :]
    rel = u_idx - blk - t_idx
    key_pos = jnp.arange(nb, dtype=jnp.int32)[:, None, None] * blk - blk + u_idx[None]
    valid = (jnp.abs(rel) <= half)[None] & (key_pos >= 0) & (key_pos < L)
    bias = rel_bias[t5_bucket(rel * dilation)].transpose(2, 0, 1).astype(jnp.float32)

    s = jnp.einsum('zhnqd,zhnkd->zhnqk', qs, kb).astype(jnp.float32) * (Dh ** -0.5)
    s = jnp.where(valid[None, None], s + bias[None, :, None], NEG)
    m = jnp.max(s, axis=-1, keepdims=True)
    pr = jnp.exp(s - m)
    l = jnp.sum(pr, axis=-1)
    o = jnp.einsum('zhnqk,zhnkd->zhnqd', pr.astype(vb.dtype), vb).astype(jnp.float32) / l[..., None]
    lse = m[..., 0] + jnp.log(l)

    o = o.reshape(Z, H, Lp, Dh)[:, :, :L].reshape(B, dilation, H, L, Dh)
    o = o.transpose(0, 3, 1, 2, 4).reshape(B, S, H, Dh)
    lse = lse.reshape(Z, H, Lp)[:, :, :L].reshape(B, dilation, H, L)
    lse = lse.transpose(0, 3, 1, 2).reshape(B, S, H)
    return o, lse


def dilated_attention_mixer(q, k, v, q_norm, k_norm, rel_bias):
    B, S, _ = q.shape
    q = rms_norm(q.reshape(B, S, ATT_HEADS, HEAD_DIM), q_norm)
    k = rms_norm(k.reshape(B, S, ATT_HEADS, HEAD_DIM), k_norm)
    v = v.reshape(B, S, ATT_HEADS, HEAD_DIM)
    outs, lses = [], []
    for window, dilation in DILATED_PAIRS:
        o, lse = dilated_window_attention(q, k, v, rel_bias, window, dilation)
        outs.append(o)
        lses.append(lse)
    w = jax.nn.softmax(jnp.stack(lses, axis=0), axis=0)
    o = jnp.sum(w[..., None] * jnp.stack(outs, axis=0), axis=0)
    return o.reshape(B, S, ATT_WIDTH).astype(q.dtype)


def conformer_conv_mixer(val, gate, conv_w, conv_b, ln_g, ln_b):
    u = val * jax.nn.sigmoid(gate)
    pad = DW_CONV_SIZE // 2
    y = lax.conv_general_dilated(u, conv_w[:, None, :].astype(u.dtype), window_strides=(1,),
                                 padding=((pad, pad),), dimension_numbers=('NWC', 'WIO', 'NWC'),
                                 feature_group_count=CONV_CHANNELS)
    y = layer_norm(y + conv_b, ln_g, ln_b)
    return jax.nn.silu(y)


def expert_choice_ffn(h, w_router, w_gate, w_up, w_down):
    B, S, D = h.shape
    cap = EC_CAPACITY * S // N_EXPERTS
    aff = jax.nn.softmax(jnp.einsum('bsd,de->bse', h, w_router).astype(jnp.float32), axis=-1)
    gate, idx = lax.top_k(aff.transpose(0, 2, 1), cap)
    xe = jax.vmap(lambda hb, ib: hb[ib])(h, idx)
    g = jnp.einsum('becd,edf->becf', xe, w_gate)
    u = jnp.einsum('becd,edf->becf', xe, w_up)
    y = jnp.einsum('becf,efd->becd', jax.nn.silu(g) * u, w_down)
    y = y * gate[..., None].astype(y.dtype)
    return jax.vmap(lambda yb, ib: jnp.zeros((S, D), yb.dtype).at[ib.reshape(-1)].add(yb.reshape(-1, D)))(y, idx)


def setup_inputs(seed: int = 0) -> dict:
    key = jax.random.key(seed)
    ks = jax.random.split(key, 24)
    f32 = jnp.float32

    def nrm(k, shape, scale):
        return jax.random.normal(k, shape, f32) * scale

    def gain(k, shape):
        return 1.0 + 0.05 * jax.random.normal(k, shape, f32)

    return {
        "x": nrm(ks[0], (BATCH, SEQ, D_MODEL), 1.0),
        "p": nrm(ks[1], (DEPTH, BATCH, SEQ, PLE_DIM), 1.0),
        "rel_bias": nrm(ks[2], (N_BUCKETS, ATT_HEADS), 0.5),
        "norm_mix": gain(ks[3], (DEPTH, D_MODEL)),
        "w_in": nrm(ks[4], (DEPTH, D_MODEL, IN_COLS), D_MODEL ** -0.5),
        "q_norm": gain(ks[5], (DEPTH, HEAD_DIM)),
        "k_norm": gain(ks[6], (DEPTH, HEAD_DIM)),
        "conv_w": nrm(ks[7], (DEPTH, DW_CONV_SIZE, CONV_CHANNELS), DW_CONV_SIZE ** -0.5),
        "conv_b": nrm(ks[8], (DEPTH, CONV_CHANNELS), 0.02),
        "conv_ln_g": gain(ks[9], (DEPTH, CONV_CHANNELS)),
        "conv_ln_b": nrm(ks[10], (DEPTH, CONV_CHANNELS), 0.02),
        "out_norm_attn": gain(ks[11], (DEPTH, ATT_WIDTH)),
        "out_norm_conv": gain(ks[12], (DEPTH, CONV_CHANNELS)),
        "w_out": nrm(ks[13], (DEPTH, MIX_WIDTH, D_MODEL), MIX_WIDTH ** -0.5),
        "norm_ffn": gain(ks[14], (DEPTH, D_MODEL)),
        "w_router": nrm(ks[15], (DEPTH, D_MODEL, N_EXPERTS), D_MODEL ** -0.5),
        "w_gate": nrm(ks[16], (DEPTH, N_EXPERTS, D_MODEL, D_FF_EXPERT), D_MODEL ** -0.5),
        "w_up": nrm(ks[17], (DEPTH, N_EXPERTS, D_MODEL, D_FF_EXPERT), D_MODEL ** -0.5),
        "w_down": nrm(ks[18], (DEPTH, N_EXPERTS, D_FF_EXPERT, D_MODEL), D_FF_EXPERT ** -0.5),
        "norm_ple": gain(ks[19], (DEPTH, D_MODEL)),
        "w_ple_gate": nrm(ks[20], (DEPTH, D_MODEL, D_MODEL), D_MODEL ** -0.5),
        "w_ple_proj": nrm(ks[21], (DEPTH, PLE_DIM, D_MODEL), PLE_DIM ** -0.5),
    }


def reference(x, p, rel_bias, norm_mix, w_in, q_norm, k_norm, conv_w, conv_b, conv_ln_g, conv_ln_b,
              out_norm_attn, out_norm_conv, w_out, norm_ffn, w_router, w_gate, w_up, w_down,
              norm_ple, w_ple_gate, w_ple_proj):
    a0, a1, a2 = ATT_WIDTH, 2 * ATT_WIDTH, 3 * ATT_WIDTH
    c0 = a2 + CONV_CHANNELS
    for i in range(DEPTH):
        h = rms_norm(x, norm_mix[i])
        z = jnp.einsum('bsd,dc->bsc', h, w_in[i])
        o_att = dilated_attention_mixer(z[..., :a0], z[..., a0:a1], z[..., a1:a2],
                                        q_norm[i], k_norm[i], rel_bias)
        o_conv = conformer_conv_mixer(z[..., a2:c0], z[..., c0:], conv_w[i], conv_b[i],
                                      conv_ln_g[i], conv_ln_b[i])
        mixed = jnp.concatenate([rms_norm(o_att, out_norm_attn[i]),
                                 rms_norm(o_conv, out_norm_conv[i])], axis=-1)
        x = x + jnp.einsum('bsm,md->bsd', mixed, w_out[i])
        x = x + expert_choice_ffn(rms_norm(x, norm_ffn[i]), w_router[i], w_gate[i], w_up[i], w_down[i])
        g = jax.nn.sigmoid(jnp.einsum('bsd,de->bse', rms_norm(x, norm_ple[i]), w_ple_gate[i]))
        x = x + g * jnp.einsum('bsq,qd->bsd', p[i].astype(x.dtype), w_ple_proj[i])
    return x
```

```python
import functools
import math

import jax
import jax.numpy as jnp
from jax import lax
from jax.experimental import pallas as pl
from jax.experimental.pallas import tpu as pltpu

F32 = jnp.float32
BF16 = jnp.bfloat16

ATT_HEADS = 8
HEAD_DIM = 128
DW_CONV_SIZE = 31
DILATED_PAIRS = ((128, 1), (512, 4), (2048, 16))
N_BUCKETS = 32
MAX_DISTANCE = 1024
EC_CAPACITY = 2
EPS = 1e-6
NEG = -1e30

LANES = 128
VMEM_LIMIT = 56 * 1024 * 1024


def _params(sem, vmem=VMEM_LIMIT):
    return pltpu.CompilerParams(dimension_semantics=sem, vmem_limit_bytes=vmem)


def _resident(shape):
    nd = len(shape)
    return pl.BlockSpec(shape, lambda *_: (0,) * nd, pipeline_mode=pl.Buffered(1))


def _rms(x, g):
    return x * lax.rsqrt(jnp.mean(x * x, axis=-1, keepdims=True) + EPS) * g


def _sigmoid(x):
    return 1.0 / (1.0 + jnp.exp(-x))


def _in_proj_kernel(x_ref, g_ref, w_ref, qn_ref, kn_ref, q_ref, k_ref, v_ref, u_ref, *, width):
    h = _rms(x_ref[...], g_ref[...]).astype(BF16)

    def proj(c):
        return jnp.dot(h, w_ref[:, c * width:(c + 1) * width], preferred_element_type=F32)

    def head_norm(z, gn, o_ref):
        for hh in range(width // HEAD_DIM):
            sl = slice(hh * HEAD_DIM, (hh + 1) * HEAD_DIM)
            o_ref[:, sl] = _rms(z[:, sl], gn).astype(o_ref.dtype)

    head_norm(proj(0), qn_ref[...], q_ref)
    head_norm(proj(1), kn_ref[...], k_ref)
    v_ref[...] = proj(2).astype(v_ref.dtype)
    u_ref[...] = proj(3) * _sigmoid(proj(4))


def _in_proj(x2d, norm_mix, w_in_bf, q_norm, k_norm, *, tm=512):
    t, d = x2d.shape
    width = w_in_bf.shape[1] // 5
    row = lambda i: (i, 0)
    outs = pl.pallas_call(
        functools.partial(_in_proj_kernel, width=width),
        grid=(t // tm,),
        in_specs=[pl.BlockSpec((tm, d), row), _resident((1, d)), _resident(w_in_bf.shape),
                  _resident((1, HEAD_DIM)), _resident((1, HEAD_DIM))],
        out_specs=[pl.BlockSpec((tm, width), row)] * 4,
        out_shape=[jax.ShapeDtypeStruct((t, width), BF16)] * 3 + [jax.ShapeDtypeStruct((t, width), F32)],
        compiler_params=_params(("arbitrary",)),
        name="in_proj",
    )(x2d, norm_mix.reshape(1, d), w_in_bf, q_norm.reshape(1, -1), k_norm.reshape(1, -1))
    return outs


ATT_REACH = max(w // 2 for w, _ in DILATED_PAIRS)


def _t5_bucket(rel):
    half_b = N_BUCKETS // 2
    exact = half_b // 2
    n = jnp.abs(rel)
    nf = jnp.maximum(n, 1).astype(F32)
    large = exact + (jnp.log(nf / exact) / math.log(MAX_DISTANCE / exact) * (half_b - exact)).astype(jnp.int32)
    large = jnp.minimum(large, half_b - 1)
    return jnp.where(rel > 0, half_b, 0) + jnp.where(n < exact, n, large)


def _attn_table(rel_bias, tq):
    nside = ATT_REACH // LANES
    nd = 2 * nside + tq // LANES
    off = (jnp.arange(nd, dtype=jnp.int32) - nside) * LANES
    d = off[:, None, None] + jnp.arange(LANES, dtype=jnp.int32)[None, None, :] \
        - jnp.arange(tq, dtype=jnp.int32)[None, :, None]
    mult = jnp.zeros(d.shape, F32)
    for window, dil in DILATED_PAIRS:
        mult = mult + ((d % dil == 0) & (jnp.abs(d) <= window // 2)).astype(F32)
    bias = rel_bias[_t5_bucket(d)].astype(F32)
    tab = jnp.where(mult[..., None] > 0, bias + jnp.log(jnp.maximum(mult, 1.0))[..., None], NEG)
    tab = jnp.transpose(tab, (3, 0, 1, 2))
    pad = jnp.full((tab.shape[0], 1, tq, LANES), NEG, F32)
    return jnp.concatenate([pad, tab, pad], axis=1)


def _attn_kernel(q_ref, k_ref, v_ref, tab_ref, o_ref, *, tq, seq, heads):
    qt = tq // LANES
    nside = ATT_REACH // LANES
    nwin = 2 * nside + qt
    nblk = seq // LANES
    qb = pl.program_id(2) * qt
    start_blk = jnp.clip(qb - nside, 0, nblk - nwin)
    start = pl.multiple_of(start_blk * LANES, LANES)
    scale = HEAD_DIM ** -0.5
    for hh in range(heads):
        sl = slice(hh * HEAD_DIM, (hh + 1) * HEAD_DIM)
        q = q_ref[0, :, sl]
        kw = k_ref[0, pl.ds(start, nwin * LANES), sl]
        s = lax.dot_general(q, kw, (((1,), (1,)), ((), ())), preferred_element_type=F32) * scale
        parts = []
        for jj in range(nwin):
            ti = jnp.clip(start_blk + jj - qb + nside + 1, 0, nwin + 1)
            parts.append(s[:, jj * LANES:(jj + 1) * LANES] + tab_ref[hh, ti])
        s = jnp.concatenate(parts, axis=-1)
        m = jnp.max(s, axis=-1, keepdims=True)
        p = jnp.exp(s - m)
        l = jnp.sum(p, axis=-1, keepdims=True)
        vw = v_ref[0, pl.ds(start, nwin * LANES), sl]
        o = jnp.dot(p.astype(BF16), vw, preferred_element_type=F32)
        o_ref[0, :, sl] = o / l


def _attention(q, k, v, rel_bias, *, tq=128, head_groups=2):
    b, s, a = q.shape
    heads = ATT_HEADS // head_groups
    gw = heads * HEAD_DIM
    tab = _attn_table(rel_bias, tq)
    nt = tab.shape[1]
    kv_spec = pl.BlockSpec((1, s, gw), lambda bi, g, i: (bi, 0, g), pipeline_mode=pl.Buffered(1))
    return pl.pallas_call(
        functools.partial(_attn_kernel, tq=tq, seq=s, heads=heads),
        grid=(b, head_groups, s // tq),
        in_specs=[pl.BlockSpec((1, tq, gw), lambda bi, g, i: (bi, i, g)), kv_spec, kv_spec,
                  pl.BlockSpec((heads, nt, tq, LANES), lambda bi, g, i: (g, 0, 0, 0),
                               pipeline_mode=pl.Buffered(1))],
        out_specs=pl.BlockSpec((1, tq, gw), lambda bi, g, i: (bi, i, g)),
        out_shape=jax.ShapeDtypeStruct((b, s, a), F32),
        compiler_params=_params(("arbitrary",) * 3),
        name="attn",
    )(q, k, v, tab)


CONV_HALO = 16
CONV_ROWS = 32
CONV_LANES = 256


def _conv_kernel(u_ref, up_ref, un_ref, cw_ref, cb_ref, lg_ref, lb_ref, og_ref, o_ref, ext_ref, y_ref,
                 *, tm, ntiles):
    i = pl.program_id(1)
    c = u_ref.shape[-1]
    pad = DW_CONV_SIZE // 2
    ext_ref[0:CONV_HALO, :] = jnp.where(i == 0, 0.0, up_ref[0])
    ext_ref[CONV_HALO:CONV_HALO + tm, :] = u_ref[0]
    ext_ref[CONV_HALO + tm:2 * CONV_HALO + tm, :] = jnp.where(i == ntiles - 1, 0.0, un_ref[0])
    for r0 in range(0, tm, CONV_ROWS):
        for c0 in range(0, c, CONV_LANES):
            cs = slice(c0, c0 + CONV_LANES)
            acc = jnp.zeros((CONV_ROWS, CONV_LANES), F32)
            for t in range(DW_CONV_SIZE):
                lo = CONV_HALO - pad + r0 + t
                acc = acc + ext_ref[lo:lo + CONV_ROWS, cs] * cw_ref[t:t + 1, cs]
            y_ref[r0:r0 + CONV_ROWS, cs] = acc
    y = y_ref[...] + cb_ref[...]
    mu = jnp.mean(y, axis=-1, keepdims=True)
    yc = y - mu
    var = jnp.mean(yc * yc, axis=-1, keepdims=True)
    yn = yc * lax.rsqrt(var + EPS) * lg_ref[...] + lb_ref[...]
    sw = yn * _sigmoid(yn)
    o_ref[0] = _rms(sw, og_ref[...]).astype(o_ref.dtype)


def _conv_mixer(u, conv_w, conv_b, ln_g, ln_b, out_g, *, tm=256):
    b, s, c = u.shape
    ntiles = s // tm
    hb = tm // CONV_HALO
    nh = s // CONV_HALO
    vec = lambda a: a.reshape(1, c)
    return pl.pallas_call(
        functools.partial(_conv_kernel, tm=tm, ntiles=ntiles),
        grid=(b, ntiles),
        in_specs=[pl.BlockSpec((1, tm, c), lambda bi, i: (bi, i, 0)),
                  pl.BlockSpec((1, CONV_HALO, c), lambda bi, i: (bi, jnp.maximum(i * hb - 1, 0), 0)),
                  pl.BlockSpec((1, CONV_HALO, c), lambda bi, i: (bi, jnp.minimum((i + 1) * hb, nh - 1), 0)),
                  _resident((DW_CONV_SIZE, c))] + [_resident((1, c))] * 4,
        out_specs=pl.BlockSpec((1, tm, c), lambda bi, i: (bi, i, 0)),
        out_shape=jax.ShapeDtypeStruct((b, s, c), BF16),
        scratch_shapes=[pltpu.VMEM((tm + 2 * CONV_HALO, c), F32), pltpu.VMEM((tm, c), F32)],
        compiler_params=_params(("arbitrary",) * 2),
        name="conv",
    )(u, u, u, conv_w, vec(conv_b), vec(ln_g), vec(ln_b), vec(out_g))


def _out_proj_kernel(oa_ref, cn_ref, x_ref, ga_ref, wo_ref, gf_ref, wr_ref, x1_ref, h2_ref, aff_ref):
    a = oa_ref.shape[-1]
    an = _rms(oa_ref[...], ga_ref[...]).astype(BF16)
    acc = jnp.dot(an, wo_ref[0:a, :], preferred_element_type=F32)
    acc = acc + jnp.dot(cn_ref[...], wo_ref[a:, :], preferred_element_type=F32)
    x1 = x_ref[...] + acc
    x1_ref[...] = x1
    h2 = _rms(x1, gf_ref[...])
    h2_ref[...] = h2
    logits = jnp.dot(h2, wr_ref[...], preferred_element_type=F32, precision=lax.Precision.HIGHEST)
    e = jnp.exp(logits - jnp.max(logits, axis=-1, keepdims=True))
    aff_ref[...] = e / jnp.sum(e, axis=-1, keepdims=True)


def _out_proj(o_att, c_n, x2d, g_att, w_out_bf, g_ffn, w_router, *, tm=512):
    t, d = x2d.shape
    a = o_att.shape[1]
    ne = w_router.shape[1]
    row = lambda i: (i, 0)
    return pl.pallas_call(
        _out_proj_kernel,
        grid=(t // tm,),
        in_specs=[pl.BlockSpec((tm, a), row), pl.BlockSpec((tm, c_n.shape[1]), row), pl.BlockSpec((tm, d), row),
                  _resident((1, a)), _resident(w_out_bf.shape), _resident((1, d)), _resident(w_router.shape)],
        out_specs=[pl.BlockSpec((tm, d), row), pl.BlockSpec((tm, d), row), pl.BlockSpec((tm, ne), row)],
        out_shape=[jax.ShapeDtypeStruct((t, d), F32), jax.ShapeDtypeStruct((t, d), F32),
                   jax.ShapeDtypeStruct((t, ne), F32)],
        compiler_params=_params(("arbitrary",)),
        name="out_proj",
    )(o_att, c_n, x2d, g_att.reshape(1, a), w_out_bf, g_ffn.reshape(1, d), w_router)


ROUTE_BLK = 128


def _route_kernel(aff_ref, idx_ref, gate_ref, m_ref, pos_ref, *, cap):
    s, ne = aff_ref.shape
    capf = float(cap)

    def keys():
        return pltpu.bitcast(aff_ref[...], jnp.int32)

    def count(mask):
        return jnp.sum(mask.astype(F32), axis=0, keepdims=True)

    def key_bit(it, thr):
        cand = thr | jnp.left_shift(jnp.int32(1), 30 - it)
        return jnp.where(count(keys() >= cand) >= capf, cand, thr)

    thr = lax.fori_loop(0, 31, key_bit, jnp.zeros((1, ne), jnp.int32))
    need = capf - count(keys() > thr)
    tok = lax.broadcasted_iota(jnp.int32, (s, ne), 0)

    tok_bits = (s - 1).bit_length()

    def tok_bit(it, lim):
        cand = lim | jnp.left_shift(jnp.int32(1), tok_bits - 1 - it)
        below = count((keys() == thr) & (tok < cand))
        return jnp.where(below < need, cand, lim)

    lim = lax.fori_loop(0, tok_bits, tok_bit, jnp.zeros((1, ne), jnp.int32))
    k = keys()
    m_ref[...] = ((k > thr) | ((k == thr) & (tok <= lim))).astype(F32)

    r = lax.broadcasted_iota(jnp.int32, (ROUTE_BLK, ROUTE_BLK), 0)
    cidx = lax.broadcasted_iota(jnp.int32, (ROUTE_BLK, ROUTE_BLK), 1)
    tri = (r > cidx).astype(BF16)

    def prefix(kb, carry):
        r0 = pl.multiple_of(kb * ROUTE_BLK, ROUTE_BLK)
        mb = m_ref[pl.ds(r0, ROUTE_BLK), :]
        local = jnp.dot(tri, mb.astype(BF16), preferred_element_type=F32)
        pos_ref[pl.ds(r0, ROUTE_BLK), :] = jnp.where(mb > 0, local + carry, -1.0)
        return carry + jnp.sum(mb, axis=0, keepdims=True)

    lax.fori_loop(0, s // ROUTE_BLK, prefix, jnp.zeros((1, ne), F32))

    nsb = cap // LANES
    lane = lax.broadcasted_iota(jnp.int32, (ROUTE_BLK, LANES), 1).astype(F32)
    sub = lax.broadcasted_iota(jnp.int32, (ROUTE_BLK, LANES), 0).astype(F32)
    for e in range(ne):
        def gather_blk(kb, accs):
            r0 = pl.multiple_of(kb * ROUTE_BLK, ROUTE_BLK)
            pos_b = jnp.broadcast_to(pos_ref[pl.ds(r0, ROUTE_BLK), e:e + 1], (ROUTE_BLK, LANES))
            aff_b = jnp.broadcast_to(aff_ref[pl.ds(r0, ROUTE_BLK), e:e + 1], (ROUTE_BLK, LANES))
            tokf = sub + jnp.asarray(kb * ROUTE_BLK, F32)
            out = []
            for sb in range(nsb):
                hit = pos_b == lane + float(sb * LANES)
                ai, ag = accs[2 * sb], accs[2 * sb + 1]
                ai = ai + jnp.sum(jnp.where(hit, tokf, 0.0).reshape(ROUTE_BLK // 8, 8, LANES), axis=0)
                ag = ag + jnp.sum(jnp.where(hit, aff_b, 0.0).reshape(ROUTE_BLK // 8, 8, LANES), axis=0)
                out += [ai, ag]
            return tuple(out)

        accs = lax.fori_loop(0, s // ROUTE_BLK, gather_blk,
                             tuple(jnp.zeros((8, LANES), F32) for _ in range(2 * nsb)))
        for sb in range(nsb):
            ls = slice(sb * LANES, (sb + 1) * LANES)
            idx_ref[e, :, ls] = jnp.sum(accs[2 * sb], axis=0, keepdims=True).astype(jnp.int32)
            gate_ref[e, :, ls] = jnp.sum(accs[2 * sb + 1], axis=0, keepdims=True)


def _route(aff, batch, cap):
    t, ne = aff.shape
    s = t // batch
    return pl.pallas_call(
        functools.partial(_route_kernel, cap=cap),
        grid=(batch,),
        in_specs=[pl.BlockSpec((s, ne), lambda bi: (bi, 0))],
        out_specs=[pl.BlockSpec((ne, 1, cap), lambda bi: (bi, 0, 0))] * 2,
        out_shape=[jax.ShapeDtypeStruct((batch * ne, 1, cap), jnp.int32),
                   jax.ShapeDtypeStruct((batch * ne, 1, cap), F32)],
        scratch_shapes=[pltpu.VMEM((s, ne), F32), pltpu.VMEM((s, ne), F32)],
        compiler_params=_params(("arbitrary",)),
        name="route",
    )(aff)


DMA_UNROLL = 8


def _experts_kernel(idx_ref, gate_ref, h2_hbm, x1_hbm, wg_ref, wu_ref, wd_ref, x2_hbm,
                    rows_ref, xe_ref, acc_ref, sem, *, seq, cap, nf):
    b = pl.program_id(1)
    f = pl.program_id(2)
    base = b * seq

    def row_copy(src_hbm, j):
        tok = idx_ref[0, 0, j]
        return pltpu.make_async_copy(src_hbm.at[pl.ds(base + tok, 1), :], rows_ref.at[pl.ds(j, 1), :], sem)

    def row_store(j):
        tok = idx_ref[0, 0, j]
        return pltpu.make_async_copy(rows_ref.at[pl.ds(j, 1), :], x2_hbm.at[pl.ds(base + tok, 1), :], sem)

    def for_rows(fn):
        def body(j, carry):
            fn(j)
            return carry
        lax.fori_loop(0, cap, body, 0, unroll=DMA_UNROLL)

    @pl.when(f == 0)
    def _():
        for_rows(lambda j: row_copy(h2_hbm, j).start())
        for_rows(lambda j: row_copy(h2_hbm, j).wait())
        xe_ref[...] = rows_ref[...].astype(BF16)
        acc_ref[...] = jnp.zeros_like(acc_ref)
        for_rows(lambda j: row_copy(x2_hbm, j).start())

    xe = xe_ref[...]
    g = jnp.dot(xe, wg_ref[0].astype(BF16), preferred_element_type=F32)
    u = jnp.dot(xe, wu_ref[0].astype(BF16), preferred_element_type=F32)
    hid = (g * _sigmoid(g) * u).astype(BF16)
    acc_ref[...] += jnp.dot(hid, wd_ref[0].astype(BF16), preferred_element_type=F32)

    @pl.when(f == nf - 1)
    def _():
        for_rows(lambda j: row_copy(x2_hbm, j).wait())
        for c0 in range(0, cap, LANES):
            gcol = jnp.broadcast_to(gate_ref[0, :, c0:c0 + LANES], (LANES, LANES)).T[:, 0:1]
            rows_ref[c0:c0 + LANES, :] += acc_ref[c0:c0 + LANES, :] * gcol
        for_rows(lambda j: row_store(j).start())
        for_rows(lambda j: row_store(j).wait())


def _experts(idx, gate, h2, x1, w_gate, w_up, w_down, *, batch, tf=256):
    t, d = x1.shape
    ne, _, dff = w_gate.shape
    cap = idx.shape[-1]
    nf = dff // tf
    return pl.pallas_call(
        functools.partial(_experts_kernel, seq=t // batch, cap=cap, nf=nf),
        grid=(ne, batch, nf),
        in_specs=[pl.BlockSpec((1, 1, cap), lambda e, b, f: (b * ne + e, 0, 0), memory_space=pltpu.SMEM),
                  pl.BlockSpec((1, 1, cap), lambda e, b, f: (b * ne + e, 0, 0)),
                  pl.BlockSpec(memory_space=pl.ANY), pl.BlockSpec(memory_space=pl.ANY),
                  pl.BlockSpec((1, d, tf), lambda e, b, f: (e, 0, f)),
                  pl.BlockSpec((1, d, tf), lambda e, b, f: (e, 0, f)),
                  pl.BlockSpec((1, tf, d), lambda e, b, f: (e, f, 0))],
        out_specs=pl.BlockSpec(memory_space=pl.ANY),
        out_shape=jax.ShapeDtypeStruct((t, d), F32),
        input_output_aliases={3: 0},
        scratch_shapes=[pltpu.VMEM((cap, d), F32), pltpu.VMEM((cap, d), BF16), pltpu.VMEM((cap, d), F32),
                        pltpu.SemaphoreType.DMA],
        compiler_params=_params(("arbitrary",) * 3),
        name="experts",
    )(idx, gate, h2, x1, w_gate, w_up, w_down)


def _ple_kernel(x_ref, p_ref, g_ref, wg_ref, wp_ref, o_ref):
    x = x_ref[...]
    hn = _rms(x, g_ref[...]).astype(BF16)
    gate = _sigmoid(jnp.dot(hn, wg_ref[...], preferred_element_type=F32))
    proj = jnp.dot(p_ref[...].astype(BF16), wp_ref[...], preferred_element_type=F32)
    o_ref[...] = x + gate * proj


def _ple(x2d, p2d, g, wg_bf, wp_bf, *, tm=512):
    t, d = x2d.shape
    q = p2d.shape[1]
    row = lambda i: (i, 0)
    return pl.pallas_call(
        _ple_kernel,
        grid=(t // tm,),
        in_specs=[pl.BlockSpec((tm, d), row), pl.BlockSpec((tm, q), row), _resident((1, d)),
                  _resident(wg_bf.shape), _resident(wp_bf.shape)],
        out_specs=pl.BlockSpec((tm, d), row),
        out_shape=jax.ShapeDtypeStruct((t, d), F32),
        compiler_params=_params(("arbitrary",)),
        name="ple",
    )(x2d, p2d, g.reshape(1, d), wg_bf, wp_bf)


def kernel(x, p, rel_bias, norm_mix, w_in, q_norm, k_norm, conv_w, conv_b, conv_ln_g, conv_ln_b,
           out_norm_attn, out_norm_conv, w_out, norm_ffn, w_router, w_gate, w_up, w_down,
           norm_ple, w_ple_gate, w_ple_proj):
    b, s, d = x.shape
    t = b * s
    ne = w_router.shape[-1]
    cap = EC_CAPACITY * s // ne
    x2d = x.reshape(t, d)
    for i in range(w_in.shape[0]):
        q, k, v, u = _in_proj(x2d, norm_mix[i], w_in[i].astype(BF16), q_norm[i], k_norm[i])
        a = q.shape[1]
        o_att = _attention(q.reshape(b, s, a), k.reshape(b, s, a), v.reshape(b, s, a), rel_bias)
        c_n = _conv_mixer(u.reshape(b, s, -1), conv_w[i], conv_b[i], conv_ln_g[i], conv_ln_b[i], out_norm_conv[i])
        x1, h2, aff = _out_proj(o_att.reshape(t, a), c_n.reshape(t, -1), x2d, out_norm_attn[i],
                                w_out[i].astype(BF16), norm_ffn[i], w_router[i])
        idx, gate = _route(aff, b, cap)
        x2 = _experts(idx, gate, h2, x1, w_gate[i], w_up[i], w_down[i], batch=b)
        x2d = _ple(x2, p[i].reshape(t, -1), norm_ple[i], w_ple_gate[i].astype(BF16), w_ple_proj[i].astype(BF16))
    return x2d.reshape(b, s, d)
```

```python
import functools
import math

import jax
import jax.numpy as jnp
from jax import lax
from jax.experimental import pallas as pl
from jax.experimental.pallas import tpu as pltpu

F32 = jnp.float32
BF16 = jnp.bfloat16

ATT_HEADS = 8
HEAD_DIM = 128
DW_CONV_SIZE = 31
DILATED_PAIRS = ((128, 1), (512, 4), (2048, 16))
N_BUCKETS = 32
MAX_DISTANCE = 1024
EC_CAPACITY = 2
EPS = 1e-6
NEG = -1e30

LANES = 128
VMEM_LIMIT = 56 * 1024 * 1024


def _params(sem, vmem=VMEM_LIMIT):
    return pltpu.CompilerParams(dimension_semantics=sem, vmem_limit_bytes=vmem)


def _resident(shape):
    nd = len(shape)
    return pl.BlockSpec(shape, lambda *_: (0,) * nd, pipeline_mode=pl.Buffered(1))


def _rms(x, g):
    return x * lax.rsqrt(jnp.mean(x * x, axis=-1, keepdims=True) + EPS) * g


def _sigmoid(x):
    return 1.0 / (1.0 + jnp.exp(-x))


def _split_bf16(x):
    hi = x.astype(BF16)
    return hi, (x - hi.astype(F32)).astype(BF16)


def _in_proj_kernel(x_ref, g_ref, w_ref, qn_ref, kn_ref, q_ref, k_ref, v_ref, u_ref, *, width):
    h = _rms(x_ref[...], g_ref[...]).astype(BF16)

    def proj(c):
        return jnp.dot(h, w_ref[:, c * width:(c + 1) * width], preferred_element_type=F32)

    def head_norm(z, gn, o_ref):
        for hh in range(width // HEAD_DIM):
            sl = slice(hh * HEAD_DIM, (hh + 1) * HEAD_DIM)
            o_ref[:, sl] = _rms(z[:, sl], gn).astype(o_ref.dtype)

    head_norm(proj(0), qn_ref[...], q_ref)
    head_norm(proj(1), kn_ref[...], k_ref)
    v_ref[...] = proj(2).astype(v_ref.dtype)
    u_ref[...] = proj(3) * _sigmoid(proj(4))


def _in_proj(x2d, norm_mix, w_in_bf, q_norm, k_norm, *, tm=512):
    t, d = x2d.shape
    width = w_in_bf.shape[1] // 5
    row = lambda i: (i, 0)
    outs = pl.pallas_call(
        functools.partial(_in_proj_kernel, width=width),
        grid=(t // tm,),
        in_specs=[pl.BlockSpec((tm, d), row), _resident((1, d)), _resident(w_in_bf.shape),
                  _resident((1, HEAD_DIM)), _resident((1, HEAD_DIM))],
        out_specs=[pl.BlockSpec((tm, width), row)] * 4,
        out_shape=[jax.ShapeDtypeStruct((t, width), BF16)] * 3 + [jax.ShapeDtypeStruct((t, width), F32)],
        compiler_params=_params(("arbitrary",)),
        name="in_proj",
    )(x2d, norm_mix.reshape(1, d), w_in_bf, q_norm.reshape(1, -1), k_norm.reshape(1, -1))
    return outs


ATT_REACH = max(w // 2 for w, _ in DILATED_PAIRS)


def _t5_bucket(rel):
    half_b = N_BUCKETS // 2
    exact = half_b // 2
    n = jnp.abs(rel)
    nf = jnp.maximum(n, 1).astype(F32)
    large = exact + (jnp.log(nf / exact) / math.log(MAX_DISTANCE / exact) * (half_b - exact)).astype(jnp.int32)
    large = jnp.minimum(large, half_b - 1)
    return jnp.where(rel > 0, half_b, 0) + jnp.where(n < exact, n, large)


def _attn_table(rel_bias, tq):
    nside = ATT_REACH // LANES + 1
    nd = 2 * nside + tq // LANES
    ncol = nd * LANES
    m = ncol + tq - 1
    d = jnp.arange(m, dtype=jnp.int32) - (nside * LANES + tq - 1)
    mult = jnp.zeros(d.shape, F32)
    for window, dil in DILATED_PAIRS:
        mult = mult + ((d % dil == 0) & (jnp.abs(d) <= window // 2)).astype(F32)
    bias = rel_bias[_t5_bucket(d)].astype(F32).T
    val = jnp.where(mult > 0, bias + jnp.log(jnp.maximum(mult, 1.0)), NEG)
    heads = val.shape[0]
    skew = jnp.tile(jnp.pad(val, ((0, 0), (0, 1))), (1, tq))[:, :tq * m].reshape(heads, tq, m)
    tab = skew[:, :, tq - 1:tq - 1 + ncol].reshape(heads, tq, nd, LANES)
    return jnp.transpose(tab, (0, 2, 1, 3))


def _attn_kernel(q_ref, k_ref, v_ref, tab_ref, o_ref, *, tq, seq, heads):
    qt = tq // LANES
    nside = ATT_REACH // LANES
    nwin = 2 * nside + qt
    nblk = seq // LANES
    qb = pl.program_id(2) * qt
    start_blk = jnp.clip(qb - nside, 0, nblk - nwin)
    start = pl.multiple_of(start_blk * LANES, LANES)
    scale = HEAD_DIM ** -0.5
    for hh in range(heads):
        sl = slice(hh * HEAD_DIM, (hh + 1) * HEAD_DIM)
        q = q_ref[0, :, sl]
        kw = k_ref[0, pl.ds(start, nwin * LANES), sl]
        s = lax.dot_general(q, kw, (((1,), (1,)), ((), ())), preferred_element_type=F32) * scale
        parts = []
        for jj in range(nwin):
            ti = jnp.clip(start_blk + jj - qb + nside + 1, 0, nwin + 1)
            parts.append(s[:, jj * LANES:(jj + 1) * LANES] + tab_ref[hh, ti])
        s = jnp.concatenate(parts, axis=-1)
        m = jnp.max(s, axis=-1, keepdims=True)
        p = jnp.exp(s - m)
        l = jnp.sum(p, axis=-1, keepdims=True)
        vw = v_ref[0, pl.ds(start, nwin * LANES), sl]
        o = jnp.dot(p.astype(BF16), vw, preferred_element_type=F32)
        o_ref[0, :, sl] = o / l


def _attention(q, k, v, rel_bias, *, tq=128, head_groups=2):
    b, s, a = q.shape
    heads = ATT_HEADS // head_groups
    gw = heads * HEAD_DIM
    tab = _attn_table(rel_bias, tq)
    nt = tab.shape[1]
    kv_spec = pl.BlockSpec((1, s, gw), lambda bi, g, i: (bi, 0, g), pipeline_mode=pl.Buffered(1))
    return pl.pallas_call(
        functools.partial(_attn_kernel, tq=tq, seq=s, heads=heads),
        grid=(b, head_groups, s // tq),
        in_specs=[pl.BlockSpec((1, tq, gw), lambda bi, g, i: (bi, i, g)), kv_spec, kv_spec,
                  pl.BlockSpec((heads, nt, tq, LANES), lambda bi, g, i: (g, 0, 0, 0),
                               pipeline_mode=pl.Buffered(1))],
        out_specs=pl.BlockSpec((1, tq, gw), lambda bi, g, i: (bi, i, g)),
        out_shape=jax.ShapeDtypeStruct((b, s, a), F32),
        compiler_params=_params(("arbitrary",) * 3),
        name="attn",
    )(q, k, v, tab)


CONV_HALO = 16
CONV_ROWS = 32
CONV_LANES = 256


def _conv_kernel(u_ref, up_ref, un_ref, cw_ref, cb_ref, lg_ref, lb_ref, og_ref, o_ref, ext_ref, y_ref,
                 *, tm, ntiles):
    i = pl.program_id(1)
    c = u_ref.shape[-1]
    pad = DW_CONV_SIZE // 2
    ext_ref[0:CONV_HALO, :] = jnp.where(i == 0, 0.0, up_ref[0])
    ext_ref[CONV_HALO:CONV_HALO + tm, :] = u_ref[0]
    ext_ref[CONV_HALO + tm:2 * CONV_HALO + tm, :] = jnp.where(i == ntiles - 1, 0.0, un_ref[0])
    for r0 in range(0, tm, CONV_ROWS):
        for c0 in range(0, c, CONV_LANES):
            cs = slice(c0, c0 + CONV_LANES)
            acc = jnp.zeros((CONV_ROWS, CONV_LANES), F32)
            for t in range(DW_CONV_SIZE):
                lo = CONV_HALO - pad + r0 + t
                acc = acc + ext_ref[lo:lo + CONV_ROWS, cs] * cw_ref[t:t + 1, cs]
            y_ref[r0:r0 + CONV_ROWS, cs] = acc
    y = y_ref[...] + cb_ref[...]
    mu = jnp.mean(y, axis=-1, keepdims=True)
    yc = y - mu
    var = jnp.mean(yc * yc, axis=-1, keepdims=True)
    yn = yc * lax.rsqrt(var + EPS) * lg_ref[...] + lb_ref[...]
    sw = yn * _sigmoid(yn)
    o_ref[0] = _rms(sw, og_ref[...]).astype(o_ref.dtype)


def _conv_mixer(u, conv_w, conv_b, ln_g, ln_b, out_g, *, tm=256):
    b, s, c = u.shape
    ntiles = s // tm
    hb = tm // CONV_HALO
    nh = s // CONV_HALO
    vec = lambda a: a.reshape(1, c)
    return pl.pallas_call(
        functools.partial(_conv_kernel, tm=tm, ntiles=ntiles),
        grid=(b, ntiles),
        in_specs=[pl.BlockSpec((1, tm, c), lambda bi, i: (bi, i, 0)),
                  pl.BlockSpec((1, CONV_HALO, c), lambda bi, i: (bi, jnp.maximum(i * hb - 1, 0), 0)),
                  pl.BlockSpec((1, CONV_HALO, c), lambda bi, i: (bi, jnp.minimum((i + 1) * hb, nh - 1), 0)),
                  _resident((DW_CONV_SIZE, c))] + [_resident((1, c))] * 4,
        out_specs=pl.BlockSpec((1, tm, c), lambda bi, i: (bi, i, 0)),
        out_shape=jax.ShapeDtypeStruct((b, s, c), BF16),
        scratch_shapes=[pltpu.VMEM((tm + 2 * CONV_HALO, c), F32), pltpu.VMEM((tm, c), F32)],
        compiler_params=_params(("arbitrary",) * 2),
        name="conv",
    )(u, u, u, conv_w, vec(conv_b), vec(ln_g), vec(ln_b), vec(out_g))


def _out_proj_kernel(oa_ref, cn_ref, x_ref, ga_ref, wo_ref, gf_ref, wr_ref, x1_ref, h2_ref, aff_ref):
    a = oa_ref.shape[-1]
    an = _rms(oa_ref[...], ga_ref[...]).astype(BF16)
    acc = jnp.dot(an, wo_ref[0:a, :], preferred_element_type=F32)
    acc = acc + jnp.dot(cn_ref[...], wo_ref[a:, :], preferred_element_type=F32)
    x1 = x_ref[...] + acc
    x1_ref[...] = x1
    h2 = _rms(x1, gf_ref[...])
    h2_ref[...] = h2
    ne = wr_ref.shape[-1]
    h_hi, h_lo = _split_bf16(h2)
    w_hi, w_lo = _split_bf16(wr_ref[...])
    t = jnp.dot(h_hi, jnp.concatenate([w_hi, w_lo], axis=1), preferred_element_type=F32)
    logits = t[:, :ne] + (t[:, ne:] + jnp.dot(h_lo, w_hi, preferred_element_type=F32))
    e = jnp.exp(logits - jnp.max(logits, axis=-1, keepdims=True))
    aff_ref[...] = e / jnp.sum(e, axis=-1, keepdims=True)


def _out_proj(o_att, c_n, x2d, g_att, w_out_bf, g_ffn, w_router, *, tm=512):
    t, d = x2d.shape
    a = o_att.shape[1]
    ne = w_router.shape[1]
    row = lambda i: (i, 0)
    return pl.pallas_call(
        _out_proj_kernel,
        grid=(t // tm,),
        in_specs=[pl.BlockSpec((tm, a), row), pl.BlockSpec((tm, c_n.shape[1]), row), pl.BlockSpec((tm, d), row),
                  _resident((1, a)), _resident(w_out_bf.shape), _resident((1, d)), _resident(w_router.shape)],
        out_specs=[pl.BlockSpec((tm, d), row), pl.BlockSpec((tm, d), row), pl.BlockSpec((tm, ne), row)],
        out_shape=[jax.ShapeDtypeStruct((t, d), F32), jax.ShapeDtypeStruct((t, d), F32),
                   jax.ShapeDtypeStruct((t, ne), F32)],
        compiler_params=_params(("arbitrary",)),
        name="out_proj",
    )(o_att, c_n, x2d, g_att.reshape(1, a), w_out_bf, g_ffn.reshape(1, d), w_router)


ROUTE_BLK = 128


def _route_kernel(aff_ref, idx_ref, gate_ref, m_ref, pos_ref, *, cap):
    s, ne = aff_ref.shape
    capf = float(cap)

    def keys():
        return pltpu.bitcast(aff_ref[...], jnp.int32)

    def count(mask):
        return jnp.sum(mask.astype(F32), axis=0, keepdims=True)

    def key_bit(it, thr):
        cand = thr | jnp.left_shift(jnp.int32(1), 30 - it)
        return jnp.where(count(keys() >= cand) >= capf, cand, thr)

    thr = lax.fori_loop(0, 31, key_bit, jnp.zeros((1, ne), jnp.int32))
    need = capf - count(keys() > thr)
    tok = lax.broadcasted_iota(jnp.int32, (s, ne), 0)

    tok_bits = (s - 1).bit_length()

    def tok_bit(it, lim):
        cand = lim | jnp.left_shift(jnp.int32(1), tok_bits - 1 - it)
        below = count((keys() == thr) & (tok < cand))
        return jnp.where(below < need, cand, lim)

    lim = lax.fori_loop(0, tok_bits, tok_bit, jnp.zeros((1, ne), jnp.int32))
    k = keys()
    m_ref[...] = ((k > thr) | ((k == thr) & (tok <= lim))).astype(F32)

    r = lax.broadcasted_iota(jnp.int32, (ROUTE_BLK, ROUTE_BLK), 0)
    cidx = lax.broadcasted_iota(jnp.int32, (ROUTE_BLK, ROUTE_BLK), 1)
    tri = (r > cidx).astype(BF16)

    def prefix(kb, carry):
        r0 = pl.multiple_of(kb * ROUTE_BLK, ROUTE_BLK)
        mb = m_ref[pl.ds(r0, ROUTE_BLK), :]
        local = jnp.dot(tri, mb.astype(BF16), preferred_element_type=F32)
        pos_ref[pl.ds(r0, ROUTE_BLK), :] = jnp.where(mb > 0, local + carry, -1.0)
        return carry + jnp.sum(mb, axis=0, keepdims=True)

    lax.fori_loop(0, s // ROUTE_BLK, prefix, jnp.zeros((1, ne), F32))

    nsb = cap // LANES
    lane = lax.broadcasted_iota(jnp.int32, (ROUTE_BLK, LANES), 1).astype(F32)
    sub = lax.broadcasted_iota(jnp.int32, (ROUTE_BLK, LANES), 0).astype(F32)
    for e in range(ne):
        def gather_blk(kb, accs):
            r0 = pl.multiple_of(kb * ROUTE_BLK, ROUTE_BLK)
            pos_b = jnp.broadcast_to(pos_ref[pl.ds(r0, ROUTE_BLK), e:e + 1], (ROUTE_BLK, LANES))
            aff_b = jnp.broadcast_to(aff_ref[pl.ds(r0, ROUTE_BLK), e:e + 1], (ROUTE_BLK, LANES))
            tokf = sub + jnp.asarray(kb * ROUTE_BLK, F32)
            out = []
            for sb in range(nsb):
                hit = pos_b == lane + float(sb * LANES)
                ai, ag = accs[2 * sb], accs[2 * sb + 1]
                ai = ai + jnp.sum(jnp.where(hit, tokf, 0.0).reshape(ROUTE_BLK // 8, 8, LANES), axis=0)
                ag = ag + jnp.sum(jnp.where(hit, aff_b, 0.0).reshape(ROUTE_BLK // 8, 8, LANES), axis=0)
                out += [ai, ag]
            return tuple(out)

        accs = lax.fori_loop(0, s // ROUTE_BLK, gather_blk,
                             tuple(jnp.zeros((8, LANES), F32) for _ in range(2 * nsb)))
        for sb in range(nsb):
            ls = slice(sb * LANES, (sb + 1) * LANES)
            idx_ref[e, :, ls] = jnp.sum(accs[2 * sb], axis=0, keepdims=True).astype(jnp.int32)
            gate_ref[e, :, ls] = jnp.sum(accs[2 * sb + 1], axis=0, keepdims=True)


def _route(aff, batch, cap):
    t, ne = aff.shape
    s = t // batch
    return pl.pallas_call(
        functools.partial(_route_kernel, cap=cap),
        grid=(batch,),
        in_specs=[pl.BlockSpec((s, ne), lambda bi: (bi, 0))],
        out_specs=[pl.BlockSpec((ne, 1, cap), lambda bi: (bi, 0, 0))] * 2,
        out_shape=[jax.ShapeDtypeStruct((batch * ne, 1, cap), jnp.int32),
                   jax.ShapeDtypeStruct((batch * ne, 1, cap), F32)],
        scratch_shapes=[pltpu.VMEM((s, ne), F32), pltpu.VMEM((s, ne), F32)],
        compiler_params=_params(("arbitrary",)),
        name="route",
    )(aff)


DMA_UNROLL = 8


def _experts_kernel(idx_ref, gate_ref, h2_hbm, x1_hbm, wg_ref, wu_ref, wd_ref, x2_hbm,
                    rows_ref, xe_ref, acc_ref, sem, *, seq, cap, nf):
    b = pl.program_id(1)
    f = pl.program_id(2)
    base = b * seq

    def row_copy(src_hbm, j):
        tok = idx_ref[0, 0, j]
        return pltpu.make_async_copy(src_hbm.at[pl.ds(base + tok, 1), :], rows_ref.at[pl.ds(j, 1), :], sem)

    def row_store(j):
        tok = idx_ref[0, 0, j]
        return pltpu.make_async_copy(rows_ref.at[pl.ds(j, 1), :], x2_hbm.at[pl.ds(base + tok, 1), :], sem)

    def for_rows(fn):
        def body(j, carry):
            fn(j)
            return carry
        lax.fori_loop(0, cap, body, 0, unroll=DMA_UNROLL)

    @pl.when(f == 0)
    def _():
        for_rows(lambda j: row_copy(h2_hbm, j).start())
        for_rows(lambda j: row_copy(h2_hbm, j).wait())
        xe_ref[...] = rows_ref[...].astype(BF16)
        acc_ref[...] = jnp.zeros_like(acc_ref)
        for_rows(lambda j: row_copy(x2_hbm, j).start())

    xe = xe_ref[...]
    g = jnp.dot(xe, wg_ref[0].astype(BF16), preferred_element_type=F32)
    u = jnp.dot(xe, wu_ref[0].astype(BF16), preferred_element_type=F32)
    hid = (g * _sigmoid(g) * u).astype(BF16)
    acc_ref[...] += jnp.dot(hid, wd_ref[0].astype(BF16), preferred_element_type=F32)

    @pl.when(f == nf - 1)
    def _():
        for_rows(lambda j: row_copy(x2_hbm, j).wait())
        for c0 in range(0, cap, LANES):
            gcol = jnp.broadcast_to(gate_ref[0, :, c0:c0 + LANES], (LANES, LANES)).T[:, 0:1]
            rows_ref[c0:c0 + LANES, :] += acc_ref[c0:c0 + LANES, :] * gcol
        for_rows(lambda j: row_store(j).start())
        for_rows(lambda j: row_store(j).wait())


def _experts(idx, gate, h2, x1, w_gate, w_up, w_down, *, batch, tf=256):
    t, d = x1.shape
    ne, _, dff = w_gate.shape
    cap = idx.shape[-1]
    nf = dff // tf
    return pl.pallas_call(
        functools.partial(_experts_kernel, seq=t // batch, cap=cap, nf=nf),
        grid=(ne, batch, nf),
        in_specs=[pl.BlockSpec((1, 1, cap), lambda e, b, f: (b * ne + e, 0, 0), memory_space=pltpu.SMEM),
                  pl.BlockSpec((1, 1, cap), lambda e, b, f: (b * ne + e, 0, 0)),
                  pl.BlockSpec(memory_space=pl.ANY), pl.BlockSpec(memory_space=pl.ANY),
                  pl.BlockSpec((1, d, tf), lambda e, b, f: (e, 0, f)),
                  pl.BlockSpec((1, d, tf), lambda e, b, f: (e, 0, f)),
                  pl.BlockSpec((1, tf, d), lambda e, b, f: (e, f, 0))],
        out_specs=pl.BlockSpec(memory_space=pl.ANY),
        out_shape=jax.ShapeDtypeStruct((t, d), F32),
        input_output_aliases={3: 0},
        scratch_shapes=[pltpu.VMEM((cap, d), F32), pltpu.VMEM((cap, d), BF16), pltpu.VMEM((cap, d), F32),
                        pltpu.SemaphoreType.DMA],
        compiler_params=_params(("arbitrary",) * 3),
        name="experts",
    )(idx, gate, h2, x1, w_gate, w_up, w_down)


def _ple_kernel(x_ref, p_ref, g_ref, wg_ref, wp_ref, o_ref):
    x = x_ref[...]
    hn = _rms(x, g_ref[...]).astype(BF16)
    gate = _sigmoid(jnp.dot(hn, wg_ref[...], preferred_element_type=F32))
    proj = jnp.dot(p_ref[...].astype(BF16), wp_ref[...], preferred_element_type=F32)
    o_ref[...] = x + gate * proj


def _ple(x2d, p2d, g, wg_bf, wp_bf, *, tm=512):
    t, d = x2d.shape
    q = p2d.shape[1]
    row = lambda i: (i, 0)
    return pl.pallas_call(
        _ple_kernel,
        grid=(t // tm,),
        in_specs=[pl.BlockSpec((tm, d), row), pl.BlockSpec((tm, q), row), _resident((1, d)),
                  _resident(wg_bf.shape), _resident(wp_bf.shape)],
        out_specs=pl.BlockSpec((tm, d), row),
        out_shape=jax.ShapeDtypeStruct((t, d), F32),
        compiler_params=_params(("arbitrary",)),
        name="ple",
    )(x2d, p2d, g.reshape(1, d), wg_bf, wp_bf)


def kernel(x, p, rel_bias, norm_mix, w_in, q_norm, k_norm, conv_w, conv_b, conv_ln_g, conv_ln_b,
           out_norm_attn, out_norm_conv, w_out, norm_ffn, w_router, w_gate, w_up, w_down,
           norm_ple, w_ple_gate, w_ple_proj):
    b, s, d = x.shape
    t = b * s
    ne = w_router.shape[-1]
    cap = EC_CAPACITY * s // ne
    x2d = x.reshape(t, d)
    for i in range(w_in.shape[0]):
        q, k, v, u = _in_proj(x2d, norm_mix[i], w_in[i].astype(BF16), q_norm[i], k_norm[i])
        a = q.shape[1]
        o_att = _attention(q.reshape(b, s, a), k.reshape(b, s, a), v.reshape(b, s, a), rel_bias)
        c_n = _conv_mixer(u.reshape(b, s, -1), conv_w[i], conv_b[i], conv_ln_g[i], conv_ln_b[i], out_norm_conv[i])
        x1, h2, aff = _out_proj(o_att.reshape(t, a), c_n.reshape(t, -1), x2d, out_norm_attn[i],
                                w_out[i].astype(BF16), norm_ffn[i], w_router[i])
        idx, gate = _route(aff, b, cap)
        x2 = _experts(idx, gate, h2, x1, w_gate[i], w_up[i], w_down[i], batch=b)
        x2d = _ple(x2, p[i].reshape(t, -1), norm_ple[i], w_ple_gate[i].astype(BF16), w_ple_proj[i].astype(BF16))
    return x2d.reshape(b, s, d)
```

```python
import functools
import math

import jax
import jax.numpy as jnp
from jax import lax
from jax.experimental import pallas as pl
from jax.experimental.pallas import tpu as pltpu

F32 = jnp.float32
BF16 = jnp.bfloat16

ATT_HEADS = 8
HEAD_DIM = 128
DW_CONV_SIZE = 31
DILATED_PAIRS = ((128, 1), (512, 4), (2048, 16))
N_BUCKETS = 32
MAX_DISTANCE = 1024
EC_CAPACITY = 2
EPS = 1e-6
NEG = -1e30

LANES = 128
VMEM_LIMIT = 56 * 1024 * 1024


def _params(sem, vmem=VMEM_LIMIT):
    return pltpu.CompilerParams(dimension_semantics=sem, vmem_limit_bytes=vmem)


def _resident(shape):
    nd = len(shape)
    return pl.BlockSpec(shape, lambda *_: (0,) * nd, pipeline_mode=pl.Buffered(1))


def _rms(x, g):
    return x * lax.rsqrt(jnp.mean(x * x, axis=-1, keepdims=True) + EPS) * g


def _sigmoid(x):
    return 1.0 / (1.0 + jnp.exp(-x))


def _split_bf16(x):
    hi = x.astype(BF16)
    return hi, (x - hi.astype(F32)).astype(BF16)


def _in_proj_kernel(x_ref, g_ref, w_ref, qn_ref, kn_ref, q_ref, k_ref, v_ref, u_ref, *, width):
    h = _rms(x_ref[...], g_ref[...]).astype(BF16)

    def proj(c):
        return jnp.dot(h, w_ref[:, c * width:(c + 1) * width], preferred_element_type=F32)

    def head_norm(z, gn, o_ref):
        for hh in range(width // HEAD_DIM):
            sl = slice(hh * HEAD_DIM, (hh + 1) * HEAD_DIM)
            o_ref[:, sl] = _rms(z[:, sl], gn).astype(o_ref.dtype)

    head_norm(proj(0), qn_ref[...], q_ref)
    head_norm(proj(1), kn_ref[...], k_ref)
    v_ref[...] = proj(2).astype(v_ref.dtype)
    u_ref[...] = proj(3) * _sigmoid(proj(4))


def _in_proj(x2d, norm_mix, w_in_bf, q_norm, k_norm, *, tm=512):
    t, d = x2d.shape
    width = w_in_bf.shape[1] // 5
    row = lambda i: (i, 0)
    outs = pl.pallas_call(
        functools.partial(_in_proj_kernel, width=width),
        grid=(t // tm,),
        in_specs=[pl.BlockSpec((tm, d), row), _resident((1, d)), _resident(w_in_bf.shape),
                  _resident((1, HEAD_DIM)), _resident((1, HEAD_DIM))],
        out_specs=[pl.BlockSpec((tm, width), row)] * 4,
        out_shape=[jax.ShapeDtypeStruct((t, width), BF16)] * 3 + [jax.ShapeDtypeStruct((t, width), F32)],
        compiler_params=_params(("arbitrary",)),
        name="in_proj",
    )(x2d, norm_mix.reshape(1, d), w_in_bf, q_norm.reshape(1, -1), k_norm.reshape(1, -1))
    return outs


ATT_REACH = max(w // 2 for w, _ in DILATED_PAIRS)


def _t5_bucket(rel):
    half_b = N_BUCKETS // 2
    exact = half_b // 2
    n = jnp.abs(rel)
    nf = jnp.maximum(n, 1).astype(F32)
    large = exact + (jnp.log(nf / exact) / math.log(MAX_DISTANCE / exact) * (half_b - exact)).astype(jnp.int32)
    large = jnp.minimum(large, half_b - 1)
    return jnp.where(rel > 0, half_b, 0) + jnp.where(n < exact, n, large)


def _attn_table(rel_bias, tq):
    nside = ATT_REACH // LANES + 1
    nd = 2 * nside + tq // LANES
    ncol = nd * LANES
    m = ncol + tq - 1
    d = jnp.arange(m, dtype=jnp.int32) - (nside * LANES + tq - 1)
    mult = jnp.zeros(d.shape, F32)
    for window, dil in DILATED_PAIRS:
        mult = mult + ((d % dil == 0) & (jnp.abs(d) <= window // 2)).astype(F32)
    bias = rel_bias[_t5_bucket(d)].astype(F32).T
    val = jnp.where(mult > 0, bias + jnp.log(jnp.maximum(mult, 1.0)), NEG)
    heads = val.shape[0]
    skew = jnp.tile(jnp.pad(val, ((0, 0), (0, 1))), (1, tq))[:, :tq * m].reshape(heads, tq, m)
    tab = skew[:, :, tq - 1:tq - 1 + ncol].reshape(heads, tq, nd, LANES)
    return jnp.transpose(tab, (0, 2, 1, 3))


def _attn_kernel(q_ref, k_ref, v_ref, tab_ref, o_ref, *, tq, seq, heads):
    qt = tq // LANES
    nside = ATT_REACH // LANES
    nwin = 2 * nside + qt
    nblk = seq // LANES
    qb = pl.program_id(2) * qt
    start_blk = jnp.clip(qb - nside, 0, nblk - nwin)
    start = pl.multiple_of(start_blk * LANES, LANES)
    scale = HEAD_DIM ** -0.5
    for hh in range(heads):
        sl = slice(hh * HEAD_DIM, (hh + 1) * HEAD_DIM)
        q = q_ref[0, :, sl]
        kw = k_ref[0, pl.ds(start, nwin * LANES), sl]
        s = lax.dot_general(q, kw, (((1,), (1,)), ((), ())), preferred_element_type=F32) * scale
        parts = []
        for jj in range(nwin):
            ti = jnp.clip(start_blk + jj - qb + nside + 1, 0, nwin + 1)
            parts.append(s[:, jj * LANES:(jj + 1) * LANES] + tab_ref[hh, ti])
        s = jnp.concatenate(parts, axis=-1)
        m = jnp.max(s, axis=-1, keepdims=True)
        p = jnp.exp(s - m)
        l = jnp.sum(p, axis=-1, keepdims=True)
        vw = v_ref[0, pl.ds(start, nwin * LANES), sl]
        o = jnp.dot(p.astype(BF16), vw, preferred_element_type=F32)
        o_ref[0, :, sl] = o / l


def _attention(q, k, v, rel_bias, *, tq=128, head_groups=2):
    b, s, a = q.shape
    heads = ATT_HEADS // head_groups
    gw = heads * HEAD_DIM
    tab = _attn_table(rel_bias, tq)
    nt = tab.shape[1]
    kv_spec = pl.BlockSpec((1, s, gw), lambda bi, g, i: (bi, 0, g), pipeline_mode=pl.Buffered(1))
    return pl.pallas_call(
        functools.partial(_attn_kernel, tq=tq, seq=s, heads=heads),
        grid=(b, head_groups, s // tq),
        in_specs=[pl.BlockSpec((1, tq, gw), lambda bi, g, i: (bi, i, g)), kv_spec, kv_spec,
                  pl.BlockSpec((heads, nt, tq, LANES), lambda bi, g, i: (g, 0, 0, 0),
                               pipeline_mode=pl.Buffered(1))],
        out_specs=pl.BlockSpec((1, tq, gw), lambda bi, g, i: (bi, i, g)),
        out_shape=jax.ShapeDtypeStruct((b, s, a), F32),
        compiler_params=_params(("arbitrary",) * 3),
        name="attn",
    )(q, k, v, tab)


CONV_HALO = 16
CONV_ROWS = 32
CONV_LANES = 256


def _conv_kernel(u_ref, up_ref, un_ref, cw_ref, cb_ref, lg_ref, lb_ref, og_ref, o_ref, ext_ref, y_ref,
                 *, tm, ntiles):
    i = pl.program_id(1)
    c = u_ref.shape[-1]
    pad = DW_CONV_SIZE // 2
    ext_ref[0:CONV_HALO, :] = jnp.where(i == 0, 0.0, up_ref[0])
    ext_ref[CONV_HALO:CONV_HALO + tm, :] = u_ref[0]
    ext_ref[CONV_HALO + tm:2 * CONV_HALO + tm, :] = jnp.where(i == ntiles - 1, 0.0, un_ref[0])
    for r0 in range(0, tm, CONV_ROWS):
        for c0 in range(0, c, CONV_LANES):
            cs = slice(c0, c0 + CONV_LANES)
            acc = jnp.zeros((CONV_ROWS, CONV_LANES), F32)
            for t in range(DW_CONV_SIZE):
                lo = CONV_HALO - pad + r0 + t
                acc = acc + ext_ref[lo:lo + CONV_ROWS, cs] * cw_ref[t:t + 1, cs]
            y_ref[r0:r0 + CONV_ROWS, cs] = acc
    y = y_ref[...] + cb_ref[...]
    mu = jnp.mean(y, axis=-1, keepdims=True)
    yc = y - mu
    var = jnp.mean(yc * yc, axis=-1, keepdims=True)
    yn = yc * lax.rsqrt(var + EPS) * lg_ref[...] + lb_ref[...]
    sw = yn * _sigmoid(yn)
    o_ref[0] = _rms(sw, og_ref[...]).astype(o_ref.dtype)


def _conv_mixer(u, conv_w, conv_b, ln_g, ln_b, out_g, *, tm=256):
    b, s, c = u.shape
    ntiles = s // tm
    hb = tm // CONV_HALO
    nh = s // CONV_HALO
    vec = lambda a: a.reshape(1, c)
    return pl.pallas_call(
        functools.partial(_conv_kernel, tm=tm, ntiles=ntiles),
        grid=(b, ntiles),
        in_specs=[pl.BlockSpec((1, tm, c), lambda bi, i: (bi, i, 0)),
                  pl.BlockSpec((1, CONV_HALO, c), lambda bi, i: (bi, jnp.maximum(i * hb - 1, 0), 0)),
                  pl.BlockSpec((1, CONV_HALO, c), lambda bi, i: (bi, jnp.minimum((i + 1) * hb, nh - 1), 0)),
                  _resident((DW_CONV_SIZE, c))] + [_resident((1, c))] * 4,
        out_specs=pl.BlockSpec((1, tm, c), lambda bi, i: (bi, i, 0)),
        out_shape=jax.ShapeDtypeStruct((b, s, c), BF16),
        scratch_shapes=[pltpu.VMEM((tm + 2 * CONV_HALO, c), F32), pltpu.VMEM((tm, c), F32)],
        compiler_params=_params(("arbitrary",) * 2),
        name="conv",
    )(u, u, u, conv_w, vec(conv_b), vec(ln_g), vec(ln_b), vec(out_g))


def _out_proj_kernel(oa_ref, cn_ref, x_ref, ga_ref, wo_ref, gf_ref, wr_ref, x1_ref, h2_ref, aff_ref):
    a = oa_ref.shape[-1]
    an = _rms(oa_ref[...], ga_ref[...]).astype(BF16)
    acc = jnp.dot(an, wo_ref[0:a, :], preferred_element_type=F32)
    acc = acc + jnp.dot(cn_ref[...], wo_ref[a:, :], preferred_element_type=F32)
    x1 = x_ref[...] + acc
    x1_ref[...] = x1
    h2 = _rms(x1, gf_ref[...])
    h2_ref[...] = h2
    ne = wr_ref.shape[-1]
    h_hi, h_lo = _split_bf16(h2)
    w_hi, w_lo = _split_bf16(wr_ref[...])
    t = jnp.dot(h_hi, jnp.concatenate([w_hi, w_lo], axis=1), preferred_element_type=F32)
    logits = t[:, :ne] + (t[:, ne:] + jnp.dot(h_lo, w_hi, preferred_element_type=F32))
    e = jnp.exp(logits - jnp.max(logits, axis=-1, keepdims=True))
    aff_ref[...] = e / jnp.sum(e, axis=-1, keepdims=True)


def _out_proj(o_att, c_n, x2d, g_att, w_out_bf, g_ffn, w_router, *, tm=512):
    t, d = x2d.shape
    a = o_att.shape[1]
    ne = w_router.shape[1]
    row = lambda i: (i, 0)
    return pl.pallas_call(
        _out_proj_kernel,
        grid=(t // tm,),
        in_specs=[pl.BlockSpec((tm, a), row), pl.BlockSpec((tm, c_n.shape[1]), row), pl.BlockSpec((tm, d), row),
                  _resident((1, a)), _resident(w_out_bf.shape), _resident((1, d)), _resident(w_router.shape)],
        out_specs=[pl.BlockSpec((tm, d), row), pl.BlockSpec((tm, d), row), pl.BlockSpec((tm, ne), row)],
        out_shape=[jax.ShapeDtypeStruct((t, d), F32), jax.ShapeDtypeStruct((t, d), F32),
                   jax.ShapeDtypeStruct((t, ne), F32)],
        compiler_params=_params(("arbitrary",)),
        name="out_proj",
    )(o_att, c_n, x2d, g_att.reshape(1, a), w_out_bf, g_ffn.reshape(1, d), w_router)


ROUTE_BLK = 128


def _route_kernel(aff_ref, idx_ref, gate_ref, m_ref, pos_ref, *, cap):
    s, ne = aff_ref.shape
    capf = float(cap)

    def keys():
        return pltpu.bitcast(aff_ref[...], jnp.int32)

    def count(mask):
        return jnp.sum(mask.astype(F32), axis=0, keepdims=True)

    def key_bit(it, thr):
        cand = thr | jnp.left_shift(jnp.int32(1), 30 - it)
        return jnp.where(count(keys() >= cand) >= capf, cand, thr)

    thr = lax.fori_loop(0, 31, key_bit, jnp.zeros((1, ne), jnp.int32))
    need = capf - count(keys() > thr)
    tok = lax.broadcasted_iota(jnp.int32, (s, ne), 0)

    tok_bits = (s - 1).bit_length()

    def tok_bit(it, lim):
        cand = lim | jnp.left_shift(jnp.int32(1), tok_bits - 1 - it)
        below = count((keys() == thr) & (tok < cand))
        return jnp.where(below < need, cand, lim)

    lim = lax.fori_loop(0, tok_bits, tok_bit, jnp.zeros((1, ne), jnp.int32))
    k = keys()
    m_ref[...] = ((k > thr) | ((k == thr) & (tok <= lim))).astype(F32)

    r = lax.broadcasted_iota(jnp.int32, (ROUTE_BLK, ROUTE_BLK), 0)
    cidx = lax.broadcasted_iota(jnp.int32, (ROUTE_BLK, ROUTE_BLK), 1)
    tri = (r > cidx).astype(BF16)

    def prefix(kb, carry):
        r0 = pl.multiple_of(kb * ROUTE_BLK, ROUTE_BLK)
        mb = m_ref[pl.ds(r0, ROUTE_BLK), :]
        local = jnp.dot(tri, mb.astype(BF16), preferred_element_type=F32)
        pos_ref[pl.ds(r0, ROUTE_BLK), :] = jnp.where(mb > 0, local + carry, -1.0)
        return carry + jnp.sum(mb, axis=0, keepdims=True)

    lax.fori_loop(0, s // ROUTE_BLK, prefix, jnp.zeros((1, ne), F32))

    nsb = cap // LANES
    lane = lax.broadcasted_iota(jnp.int32, (ROUTE_BLK, LANES), 1).astype(F32)
    sub = lax.broadcasted_iota(jnp.int32, (ROUTE_BLK, LANES), 0).astype(F32)
    for e in range(ne):
        def gather_blk(kb, accs):
            r0 = pl.multiple_of(kb * ROUTE_BLK, ROUTE_BLK)
            pos_b = jnp.broadcast_to(pos_ref[pl.ds(r0, ROUTE_BLK), e:e + 1], (ROUTE_BLK, LANES))
            aff_b = jnp.broadcast_to(aff_ref[pl.ds(r0, ROUTE_BLK), e:e + 1], (ROUTE_BLK, LANES))
            tokf = sub + jnp.asarray(kb * ROUTE_BLK, F32)
            out = []
            for sb in range(nsb):
                hit = pos_b == lane + float(sb * LANES)
                ai, ag = accs[2 * sb], accs[2 * sb + 1]
                ai = ai + jnp.sum(jnp.where(hit, tokf, 0.0).reshape(ROUTE_BLK // 8, 8, LANES), axis=0)
                ag = ag + jnp.sum(jnp.where(hit, aff_b, 0.0).reshape(ROUTE_BLK // 8, 8, LANES), axis=0)
                out += [ai, ag]
            return tuple(out)

        accs = lax.fori_loop(0, s // ROUTE_BLK, gather_blk,
                             tuple(jnp.zeros((8, LANES), F32) for _ in range(2 * nsb)))
        for sb in range(nsb):
            ls = slice(sb * LANES, (sb + 1) * LANES)
            idx_ref[e, :, ls] = jnp.sum(accs[2 * sb], axis=0, keepdims=True).astype(jnp.int32)
            gate_ref[e, :, ls] = jnp.sum(accs[2 * sb + 1], axis=0, keepdims=True)


def _route(aff, batch, cap):
    t, ne = aff.shape
    s = t // batch
    return pl.pallas_call(
        functools.partial(_route_kernel, cap=cap),
        grid=(batch,),
        in_specs=[pl.BlockSpec((s, ne), lambda bi: (bi, 0))],
        out_specs=[pl.BlockSpec((ne, 1, cap), lambda bi: (bi, 0, 0))] * 2,
        out_shape=[jax.ShapeDtypeStruct((batch * ne, 1, cap), jnp.int32),
                   jax.ShapeDtypeStruct((batch * ne, 1, cap), F32)],
        scratch_shapes=[pltpu.VMEM((s, ne), F32), pltpu.VMEM((s, ne), F32)],
        compiler_params=_params(("arbitrary",)),
        name="route",
    )(aff)


DMA_UNROLL = 8


def _experts_kernel(idx_ref, idxn_ref, gate_ref, h2_hbm, x1_hbm, wg_ref, wu_ref, wd_ref, x2_hbm,
                    stage_ref, rows_ref, xe_ref, acc_ref, sem_h, sem_r, sem_w, *, seq, cap, nf, batch, ngroups):
    b = pl.program_id(1)
    f = pl.program_id(2)
    g = pl.program_id(0) * batch + b
    base = b * seq
    base_n = lax.rem(b + 1, batch) * seq
    per = cap // nf

    def h2_copy(iref, row_base, j):
        return pltpu.make_async_copy(h2_hbm.at[pl.ds(row_base + iref[0, 0, j], 1), :],
                                     stage_ref.at[pl.ds(j, 1), :], sem_h)

    def x2_copy(j):
        return pltpu.make_async_copy(x2_hbm.at[pl.ds(base + idx_ref[0, 0, j], 1), :],
                                     rows_ref.at[pl.ds(j, 1), :], sem_r)

    def x2_store(j):
        return pltpu.make_async_copy(rows_ref.at[pl.ds(j, 1), :],
                                     x2_hbm.at[pl.ds(base + idx_ref[0, 0, j], 1), :], sem_w)

    def for_rows(fn):
        def body(j, carry):
            fn(j)
            return carry
        lax.fori_loop(0, cap, body, 0, unroll=DMA_UNROLL)

    @pl.when((g == 0) & (f == 0))
    def _():
        for_rows(lambda j: h2_copy(idx_ref, base, j).start())

    @pl.when(f == 0)
    def _():
        for_rows(lambda j: h2_copy(idx_ref, base, j).wait())
        xe_ref[...] = stage_ref[...].astype(BF16)
        acc_ref[...] = jnp.zeros_like(acc_ref)

    for jj in range(per):
        j = f * per + jj
        h2_copy(idxn_ref, base_n, j).start()
        x2_copy(j).start()

    xe = xe_ref[...]
    gg = jnp.dot(xe, wg_ref[0].astype(BF16), preferred_element_type=F32)
    uu = jnp.dot(xe, wu_ref[0].astype(BF16), preferred_element_type=F32)
    hid = (gg * _sigmoid(gg) * uu).astype(BF16)
    acc_ref[...] += jnp.dot(hid, wd_ref[0].astype(BF16), preferred_element_type=F32)

    @pl.when(f == nf - 1)
    def _():
        for_rows(lambda j: x2_copy(j).wait())
        for c0 in range(0, cap, LANES):
            gcol = jnp.broadcast_to(gate_ref[0, :, c0:c0 + LANES], (LANES, LANES)).T[:, 0:1]
            rows_ref[c0:c0 + LANES, :] += acc_ref[c0:c0 + LANES, :] * gcol
        for_rows(lambda j: x2_store(j).start())
        for_rows(lambda j: x2_store(j).wait())

    @pl.when((g == ngroups - 1) & (f == nf - 1))
    def _():
        for_rows(lambda j: h2_copy(idxn_ref, base_n, j).wait())


def _experts(idx, gate, h2, x1, w_gate, w_up, w_down, *, batch, tf=256):
    t, d = x1.shape
    ne, _, dff = w_gate.shape
    cap = idx.shape[-1]
    nf = dff // tf
    ngroups = ne * batch
    assert cap % nf == 0

    def cur(e, b, f):
        return (b * ne + e, 0, 0)

    def nxt(e, b, f):
        gn = jnp.minimum(e * batch + b + 1, ngroups - 1)
        return ((gn % batch) * ne + gn // batch, 0, 0)

    return pl.pallas_call(
        functools.partial(_experts_kernel, seq=t // batch, cap=cap, nf=nf, batch=batch, ngroups=ngroups),
        grid=(ne, batch, nf),
        in_specs=[pl.BlockSpec((1, 1, cap), cur, memory_space=pltpu.SMEM),
                  pl.BlockSpec((1, 1, cap), nxt, memory_space=pltpu.SMEM),
                  pl.BlockSpec((1, 1, cap), cur),
                  pl.BlockSpec(memory_space=pl.ANY), pl.BlockSpec(memory_space=pl.ANY),
                  pl.BlockSpec((1, d, tf), lambda e, b, f: (e, 0, f)),
                  pl.BlockSpec((1, d, tf), lambda e, b, f: (e, 0, f)),
                  pl.BlockSpec((1, tf, d), lambda e, b, f: (e, f, 0))],
        out_specs=pl.BlockSpec(memory_space=pl.ANY),
        out_shape=jax.ShapeDtypeStruct((t, d), F32),
        input_output_aliases={4: 0},
        scratch_shapes=[pltpu.VMEM((cap, d), F32), pltpu.VMEM((cap, d), F32), pltpu.VMEM((cap, d), BF16),
                        pltpu.VMEM((cap, d), F32), pltpu.SemaphoreType.DMA, pltpu.SemaphoreType.DMA,
                        pltpu.SemaphoreType.DMA],
        compiler_params=_params(("arbitrary",) * 3),
        name="experts",
    )(idx, idx, gate, h2, x1, w_gate, w_up, w_down)


def _ple_kernel(x_ref, p_ref, g_ref, wg_ref, wp_ref, o_ref):
    x = x_ref[...]
    hn = _rms(x, g_ref[...]).astype(BF16)
    gate = _sigmoid(jnp.dot(hn, wg_ref[...], preferred_element_type=F32))
    proj = jnp.dot(p_ref[...].astype(BF16), wp_ref[...], preferred_element_type=F32)
    o_ref[...] = x + gate * proj


def _ple(x2d, p2d, g, wg_bf, wp_bf, *, tm=512):
    t, d = x2d.shape
    q = p2d.shape[1]
    row = lambda i: (i, 0)
    return pl.pallas_call(
        _ple_kernel,
        grid=(t // tm,),
        in_specs=[pl.BlockSpec((tm, d), row), pl.BlockSpec((tm, q), row), _resident((1, d)),
                  _resident(wg_bf.shape), _resident(wp_bf.shape)],
        out_specs=pl.BlockSpec((tm, d), row),
        out_shape=jax.ShapeDtypeStruct((t, d), F32),
        compiler_params=_params(("arbitrary",)),
        name="ple",
    )(x2d, p2d, g.reshape(1, d), wg_bf, wp_bf)


def kernel(x, p, rel_bias, norm_mix, w_in, q_norm, k_norm, conv_w, conv_b, conv_ln_g, conv_ln_b,
           out_norm_attn, out_norm_conv, w_out, norm_ffn, w_router, w_gate, w_up, w_down,
           norm_ple, w_ple_gate, w_ple_proj):
    b, s, d = x.shape
    t = b * s
    ne = w_router.shape[-1]
    cap = EC_CAPACITY * s // ne
    x2d = x.reshape(t, d)
    for i in range(w_in.shape[0]):
        q, k, v, u = _in_proj(x2d, norm_mix[i], w_in[i].astype(BF16), q_norm[i], k_norm[i])
        a = q.shape[1]
        o_att = _attention(q.reshape(b, s, a), k.reshape(b, s, a), v.reshape(b, s, a), rel_bias)
        c_n = _conv_mixer(u.reshape(b, s, -1), conv_w[i], conv_b[i], conv_ln_g[i], conv_ln_b[i], out_norm_conv[i])
        x1, h2, aff = _out_proj(o_att.reshape(t, a), c_n.reshape(t, -1), x2d, out_norm_attn[i],
                                w_out[i].astype(BF16), norm_ffn[i], w_router[i])
        idx, gate = _route(aff, b, cap)
        x2 = _experts(idx, gate, h2, x1, w_gate[i], w_up[i], w_down[i], batch=b)
        x2d = _ple(x2, p[i].reshape(t, -1), norm_ple[i], w_ple_gate[i].astype(BF16), w_ple_proj[i].astype(BF16))
    return x2d.reshape(b, s, d)
```

```python
import functools
import math

import jax
import jax.numpy as jnp
from jax import lax
from jax.experimental import pallas as pl
from jax.experimental.pallas import tpu as pltpu

F32 = jnp.float32
BF16 = jnp.bfloat16

ATT_HEADS = 8
HEAD_DIM = 128
DW_CONV_SIZE = 31
DILATED_PAIRS = ((128, 1), (512, 4), (2048, 16))
N_BUCKETS = 32
MAX_DISTANCE = 1024
EC_CAPACITY = 2
EPS = 1e-6
NEG = -1e30

LANES = 128
VMEM_LIMIT = 56 * 1024 * 1024


def _params(sem, vmem=VMEM_LIMIT):
    return pltpu.CompilerParams(dimension_semantics=sem, vmem_limit_bytes=vmem)


def _resident(shape):
    nd = len(shape)
    return pl.BlockSpec(shape, lambda *_: (0,) * nd, pipeline_mode=pl.Buffered(1))


def _rms(x, g):
    return x * lax.rsqrt(jnp.mean(x * x, axis=-1, keepdims=True) + EPS) * g


def _sigmoid(x):
    return 1.0 / (1.0 + jnp.exp(-x))


def _split_bf16(x):
    hi = x.astype(BF16)
    return hi, (x - hi.astype(F32)).astype(BF16)


def _in_proj_kernel(x_ref, g_ref, w_ref, qn_ref, kn_ref, q_ref, k_ref, v_ref, u_ref, *, width):
    h = _rms(x_ref[...], g_ref[...]).astype(BF16)

    def proj(c):
        return jnp.dot(h, w_ref[:, c * width:(c + 1) * width], preferred_element_type=F32)

    def head_norm(z, gn, o_ref):
        for hh in range(width // HEAD_DIM):
            sl = slice(hh * HEAD_DIM, (hh + 1) * HEAD_DIM)
            o_ref[:, sl] = _rms(z[:, sl], gn).astype(o_ref.dtype)

    head_norm(proj(0), qn_ref[...], q_ref)
    head_norm(proj(1), kn_ref[...], k_ref)
    v_ref[...] = proj(2).astype(v_ref.dtype)
    u_ref[...] = proj(3) * _sigmoid(proj(4))


def _in_proj(x2d, norm_mix, w_in_bf, q_norm, k_norm, *, tm=512):
    t, d = x2d.shape
    width = w_in_bf.shape[1] // 5
    row = lambda i: (i, 0)
    outs = pl.pallas_call(
        functools.partial(_in_proj_kernel, width=width),
        grid=(t // tm,),
        in_specs=[pl.BlockSpec((tm, d), row), _resident((1, d)), _resident(w_in_bf.shape),
                  _resident((1, HEAD_DIM)), _resident((1, HEAD_DIM))],
        out_specs=[pl.BlockSpec((tm, width), row)] * 4,
        out_shape=[jax.ShapeDtypeStruct((t, width), BF16)] * 3 + [jax.ShapeDtypeStruct((t, width), F32)],
        compiler_params=_params(("arbitrary",)),
        name="in_proj",
    )(x2d, norm_mix.reshape(1, d), w_in_bf, q_norm.reshape(1, -1), k_norm.reshape(1, -1))
    return outs


ATT_REACH = max(w // 2 for w, _ in DILATED_PAIRS)


def _t5_bucket(rel):
    half_b = N_BUCKETS // 2
    exact = half_b // 2
    n = jnp.abs(rel)
    nf = jnp.maximum(n, 1).astype(F32)
    large = exact + (jnp.log(nf / exact) / math.log(MAX_DISTANCE / exact) * (half_b - exact)).astype(jnp.int32)
    large = jnp.minimum(large, half_b - 1)
    return jnp.where(rel > 0, half_b, 0) + jnp.where(n < exact, n, large)


def _attn_table(rel_bias, tq):
    nside = ATT_REACH // LANES + 1
    nd = 2 * nside + tq // LANES
    ncol = nd * LANES
    m = ncol + tq - 1
    d = jnp.arange(m, dtype=jnp.int32) - (nside * LANES + tq - 1)
    mult = jnp.zeros(d.shape, F32)
    for window, dil in DILATED_PAIRS:
        mult = mult + ((d % dil == 0) & (jnp.abs(d) <= window // 2)).astype(F32)
    bias = rel_bias[_t5_bucket(d)].astype(F32).T
    val = jnp.where(mult > 0, bias + jnp.log(jnp.maximum(mult, 1.0)), NEG)
    heads = val.shape[0]
    skew = jnp.tile(jnp.pad(val, ((0, 0), (0, 1))), (1, tq))[:, :tq * m].reshape(heads, tq, m)
    tab = skew[:, :, tq - 1:tq - 1 + ncol].reshape(heads, tq, nd, LANES)
    return jnp.transpose(tab, (0, 2, 1, 3))


def _attn_kernel(q_ref, k_ref, v_ref, tab_ref, o_ref, *, tq, seq, heads):
    qt = tq // LANES
    nside = ATT_REACH // LANES
    nwin = 2 * nside + qt
    nblk = seq // LANES
    qb = pl.program_id(2) * qt
    start_blk = jnp.clip(qb - nside, 0, nblk - nwin)
    start = pl.multiple_of(start_blk * LANES, LANES)
    scale = HEAD_DIM ** -0.5
    for hh in range(heads):
        sl = slice(hh * HEAD_DIM, (hh + 1) * HEAD_DIM)
        q = q_ref[0, :, sl]
        kw = k_ref[0, pl.ds(start, nwin * LANES), sl]
        s = lax.dot_general(q, kw, (((1,), (1,)), ((), ())), preferred_element_type=F32) * scale
        parts = []
        for jj in range(nwin):
            ti = jnp.clip(start_blk + jj - qb + nside + 1, 0, nwin + 1)
            parts.append(s[:, jj * LANES:(jj + 1) * LANES] + tab_ref[hh, ti])
        s = jnp.concatenate(parts, axis=-1)
        m = jnp.max(s, axis=-1, keepdims=True)
        p = jnp.exp(s - m)
        l = jnp.sum(p, axis=-1, keepdims=True)
        vw = v_ref[0, pl.ds(start, nwin * LANES), sl]
        o = jnp.dot(p.astype(BF16), vw, preferred_element_type=F32)
        o_ref[0, :, sl] = o / l


def _attention(q, k, v, rel_bias, *, tq=256, head_groups=2):
    b, s, a = q.shape
    heads = ATT_HEADS // head_groups
    gw = heads * HEAD_DIM
    tab = _attn_table(rel_bias, tq)
    nt = tab.shape[1]
    kv_spec = pl.BlockSpec((1, s, gw), lambda bi, g, i: (bi, 0, g), pipeline_mode=pl.Buffered(1))
    return pl.pallas_call(
        functools.partial(_attn_kernel, tq=tq, seq=s, heads=heads),
        grid=(b, head_groups, s // tq),
        in_specs=[pl.BlockSpec((1, tq, gw), lambda bi, g, i: (bi, i, g)), kv_spec, kv_spec,
                  pl.BlockSpec((heads, nt, tq, LANES), lambda bi, g, i: (g, 0, 0, 0),
                               pipeline_mode=pl.Buffered(1))],
        out_specs=pl.BlockSpec((1, tq, gw), lambda bi, g, i: (bi, i, g)),
        out_shape=jax.ShapeDtypeStruct((b, s, a), F32),
        compiler_params=_params(("arbitrary",) * 3),
        name="attn",
    )(q, k, v, tab)


CONV_HALO = 16
CONV_ROWS = 32
CONV_LANES = 256


def _conv_kernel(u_ref, up_ref, un_ref, cw_ref, cb_ref, lg_ref, lb_ref, og_ref, o_ref, ext_ref, y_ref,
                 *, tm, ntiles):
    i = pl.program_id(1)
    c = u_ref.shape[-1]
    pad = DW_CONV_SIZE // 2
    ext_ref[0:CONV_HALO, :] = jnp.where(i == 0, 0.0, up_ref[0])
    ext_ref[CONV_HALO:CONV_HALO + tm, :] = u_ref[0]
    ext_ref[CONV_HALO + tm:2 * CONV_HALO + tm, :] = jnp.where(i == ntiles - 1, 0.0, un_ref[0])
    for r0 in range(0, tm, CONV_ROWS):
        for c0 in range(0, c, CONV_LANES):
            cs = slice(c0, c0 + CONV_LANES)
            acc = jnp.zeros((CONV_ROWS, CONV_LANES), F32)
            for t in range(DW_CONV_SIZE):
                lo = CONV_HALO - pad + r0 + t
                acc = acc + ext_ref[lo:lo + CONV_ROWS, cs] * cw_ref[t:t + 1, cs]
            y_ref[r0:r0 + CONV_ROWS, cs] = acc
    y = y_ref[...] + cb_ref[...]
    mu = jnp.mean(y, axis=-1, keepdims=True)
    yc = y - mu
    var = jnp.mean(yc * yc, axis=-1, keepdims=True)
    yn = yc * lax.rsqrt(var + EPS) * lg_ref[...] + lb_ref[...]
    sw = yn * _sigmoid(yn)
    o_ref[0] = _rms(sw, og_ref[...]).astype(o_ref.dtype)


def _conv_mixer(u, conv_w, conv_b, ln_g, ln_b, out_g, *, tm=256):
    b, s, c = u.shape
    ntiles = s // tm
    hb = tm // CONV_HALO
    nh = s // CONV_HALO
    vec = lambda a: a.reshape(1, c)
    return pl.pallas_call(
        functools.partial(_conv_kernel, tm=tm, ntiles=ntiles),
        grid=(b, ntiles),
        in_specs=[pl.BlockSpec((1, tm, c), lambda bi, i: (bi, i, 0)),
                  pl.BlockSpec((1, CONV_HALO, c), lambda bi, i: (bi, jnp.maximum(i * hb - 1, 0), 0)),
                  pl.BlockSpec((1, CONV_HALO, c), lambda bi, i: (bi, jnp.minimum((i + 1) * hb, nh - 1), 0)),
                  _resident((DW_CONV_SIZE, c))] + [_resident((1, c))] * 4,
        out_specs=pl.BlockSpec((1, tm, c), lambda bi, i: (bi, i, 0)),
        out_shape=jax.ShapeDtypeStruct((b, s, c), BF16),
        scratch_shapes=[pltpu.VMEM((tm + 2 * CONV_HALO, c), F32), pltpu.VMEM((tm, c), F32)],
        compiler_params=_params(("arbitrary",) * 2),
        name="conv",
    )(u, u, u, conv_w, vec(conv_b), vec(ln_g), vec(ln_b), vec(out_g))


def _out_proj_kernel(oa_ref, cn_ref, x_ref, ga_ref, wo_ref, gf_ref, wr_ref, x1_ref, h2_ref, aff_ref):
    a = oa_ref.shape[-1]
    an = _rms(oa_ref[...], ga_ref[...]).astype(BF16)
    acc = jnp.dot(an, wo_ref[0:a, :], preferred_element_type=F32)
    acc = acc + jnp.dot(cn_ref[...], wo_ref[a:, :], preferred_element_type=F32)
    x1 = x_ref[...] + acc
    x1_ref[...] = x1
    h2 = _rms(x1, gf_ref[...])
    h2_ref[...] = h2
    ne = wr_ref.shape[-1]
    h_hi, h_lo = _split_bf16(h2)
    w_hi, w_lo = _split_bf16(wr_ref[...])
    t = jnp.dot(h_hi, jnp.concatenate([w_hi, w_lo], axis=1), preferred_element_type=F32)
    logits = t[:, :ne] + (t[:, ne:] + jnp.dot(h_lo, w_hi, preferred_element_type=F32))
    e = jnp.exp(logits - jnp.max(logits, axis=-1, keepdims=True))
    aff_ref[...] = e / jnp.sum(e, axis=-1, keepdims=True)


def _out_proj(o_att, c_n, x2d, g_att, w_out_bf, g_ffn, w_router, *, tm=512):
    t, d = x2d.shape
    a = o_att.shape[1]
    ne = w_router.shape[1]
    row = lambda i: (i, 0)
    return pl.pallas_call(
        _out_proj_kernel,
        grid=(t // tm,),
        in_specs=[pl.BlockSpec((tm, a), row), pl.BlockSpec((tm, c_n.shape[1]), row), pl.BlockSpec((tm, d), row),
                  _resident((1, a)), _resident(w_out_bf.shape), _resident((1, d)), _resident(w_router.shape)],
        out_specs=[pl.BlockSpec((tm, d), row), pl.BlockSpec((tm, d), row), pl.BlockSpec((tm, ne), row)],
        out_shape=[jax.ShapeDtypeStruct((t, d), F32), jax.ShapeDtypeStruct((t, d), F32),
                   jax.ShapeDtypeStruct((t, ne), F32)],
        compiler_params=_params(("arbitrary",)),
        name="out_proj",
    )(o_att, c_n, x2d, g_att.reshape(1, a), w_out_bf, g_ffn.reshape(1, d), w_router)


ROUTE_BLK = 128


def _route_kernel(aff_ref, pos_ref, cnt_ref, m_ref, *, cap):
    s, ne = aff_ref.shape
    capf = float(cap)

    def keys():
        return pltpu.bitcast(aff_ref[...], jnp.int32)

    def count(mask):
        return jnp.sum(mask.astype(F32), axis=0, keepdims=True)

    def key_bit(it, thr):
        cand = thr | jnp.left_shift(jnp.int32(1), 30 - it)
        return jnp.where(count(keys() >= cand) >= capf, cand, thr)

    thr = lax.fori_loop(0, 31, key_bit, jnp.zeros((1, ne), jnp.int32))
    need = capf - count(keys() > thr)
    tok = lax.broadcasted_iota(jnp.int32, (s, ne), 0)

    tok_bits = (s - 1).bit_length()

    def tok_bit(it, lim):
        cand = lim | jnp.left_shift(jnp.int32(1), tok_bits - 1 - it)
        below = count((keys() == thr) & (tok < cand))
        return jnp.where(below < need, cand, lim)

    lim = lax.fori_loop(0, tok_bits, tok_bit, jnp.zeros((1, ne), jnp.int32))
    k = keys()
    m_ref[...] = ((k > thr) | ((k == thr) & (tok <= lim))).astype(F32)

    r = lax.broadcasted_iota(jnp.int32, (ROUTE_BLK, ROUTE_BLK), 0)
    cidx = lax.broadcasted_iota(jnp.int32, (ROUTE_BLK, ROUTE_BLK), 1)
    tri = (r > cidx).astype(BF16)

    def prefix(kb, carry):
        r0 = pl.multiple_of(kb * ROUTE_BLK, ROUTE_BLK)
        mb = m_ref[pl.ds(r0, ROUTE_BLK), :]
        local = jnp.dot(tri, mb.astype(BF16), preferred_element_type=F32)
        pos_ref[pl.ds(r0, ROUTE_BLK), :] = jnp.where(mb > 0, local + carry, -1.0)
        cnt_ref[pl.ds(kb, 1), :] = carry.astype(jnp.int32)
        return carry + jnp.sum(mb, axis=0, keepdims=True)

    lax.fori_loop(0, s // ROUTE_BLK, prefix, jnp.zeros((1, ne), F32))


def _route_lists_kernel(cnt_ref, pos_ref, aff_ref, idx_ref, gate_ref, ai_ref, ag_ref, *, cap):
    bi = pl.program_id(0)
    s, ne = pos_ref.shape
    nblk = s // ROUTE_BLK
    nsb = cap // LANES
    ai_ref[...] = jnp.zeros_like(ai_ref)
    ag_ref[...] = jnp.zeros_like(ag_ref)
    lane = lax.broadcasted_iota(jnp.int32, (ROUTE_BLK, LANES), 1).astype(F32)
    sub = lax.broadcasted_iota(jnp.int32, (ROUTE_BLK, LANES), 0).astype(F32)

    def fold(x):
        return jnp.sum(x.reshape(ROUTE_BLK // 8, 8, LANES), axis=0)

    for e in range(ne):
        def blk(kb, carry):
            r0 = pl.multiple_of(kb * ROUTE_BLK, ROUTE_BLK)
            before = cnt_ref[(bi * nblk + kb) * ne + e]
            grp = jnp.minimum(lax.shift_right_logical(before, LANES.bit_length() - 1), nsb - 1)
            rel = jnp.broadcast_to(pos_ref[pl.ds(r0, ROUTE_BLK), e:e + 1], (ROUTE_BLK, LANES)) \
                - jnp.asarray(grp * LANES, F32)
            aff_b = jnp.broadcast_to(aff_ref[pl.ds(r0, ROUTE_BLK), e:e + 1], (ROUTE_BLK, LANES))
            tokf = sub + jnp.asarray(kb * ROUTE_BLK, F32)
            row = e * (nsb + 1) + grp
            for half in range(2):
                hit = rel == lane + float(half * LANES)
                ai_ref[row + half] += fold(jnp.where(hit, tokf, 0.0))
                ag_ref[row + half] += fold(jnp.where(hit, aff_b, 0.0))
            return carry

        lax.fori_loop(0, nblk, blk, 0)
    for e in range(ne):
        for sb in range(nsb):
            ls = slice(sb * LANES, (sb + 1) * LANES)
            row = e * (nsb + 1) + sb
            idx_ref[e, :, ls] = jnp.sum(ai_ref[row], axis=0, keepdims=True).astype(jnp.int32)
            gate_ref[e, :, ls] = jnp.sum(ag_ref[row], axis=0, keepdims=True)


def _route(aff, batch, cap):
    t, ne = aff.shape
    s = t // batch
    nblk = s // ROUTE_BLK
    pos, cnt = pl.pallas_call(
        functools.partial(_route_kernel, cap=cap),
        grid=(batch,),
        in_specs=[pl.BlockSpec((s, ne), lambda bi: (bi, 0))],
        out_specs=[pl.BlockSpec((s, ne), lambda bi: (bi, 0)), pl.BlockSpec((nblk, ne), lambda bi: (bi, 0))],
        out_shape=[jax.ShapeDtypeStruct((t, ne), F32), jax.ShapeDtypeStruct((batch * nblk, ne), jnp.int32)],
        scratch_shapes=[pltpu.VMEM((s, ne), F32)],
        compiler_params=_params(("arbitrary",)),
        name="route_select",
    )(aff)
    nrows = ne * (cap // LANES + 1)
    return pl.pallas_call(
        functools.partial(_route_lists_kernel, cap=cap),
        grid_spec=pltpu.PrefetchScalarGridSpec(
            num_scalar_prefetch=1,
            grid=(batch,),
            in_specs=[pl.BlockSpec((s, ne), lambda bi, c: (bi, 0)), pl.BlockSpec((s, ne), lambda bi, c: (bi, 0))],
            out_specs=[pl.BlockSpec((ne, 1, cap), lambda bi, c: (bi, 0, 0))] * 2,
            scratch_shapes=[pltpu.VMEM((nrows, 8, LANES), F32), pltpu.VMEM((nrows, 8, LANES), F32)]),
        out_shape=[jax.ShapeDtypeStruct((batch * ne, 1, cap), jnp.int32),
                   jax.ShapeDtypeStruct((batch * ne, 1, cap), F32)],
        compiler_params=_params(("arbitrary",)),
        name="route_lists",
    )(cnt.reshape(-1), pos, aff)


DMA_UNROLL = 8


def _experts_kernel(idx_ref, idxn_ref, gate_ref, h2_hbm, x1_hbm, wg_ref, wu_ref, wd_ref, x2_hbm,
                    stage_ref, rows_ref, xe_ref, acc_ref, sem_h, sem_r, sem_w, *, seq, cap, nf, batch, ngroups):
    b = pl.program_id(1)
    f = pl.program_id(2)
    g = pl.program_id(0) * batch + b
    base = b * seq
    base_n = lax.rem(b + 1, batch) * seq
    per = cap // nf

    def h2_copy(iref, row_base, j):
        return pltpu.make_async_copy(h2_hbm.at[pl.ds(row_base + iref[0, 0, j], 1), :],
                                     stage_ref.at[pl.ds(j, 1), :], sem_h)

    def x2_copy(j):
        return pltpu.make_async_copy(x2_hbm.at[pl.ds(base + idx_ref[0, 0, j], 1), :],
                                     rows_ref.at[pl.ds(j, 1), :], sem_r)

    def x2_store(j):
        return pltpu.make_async_copy(rows_ref.at[pl.ds(j, 1), :],
                                     x2_hbm.at[pl.ds(base + idx_ref[0, 0, j], 1), :], sem_w)

    def for_rows(fn):
        def body(j, carry):
            fn(j)
            return carry
        lax.fori_loop(0, cap, body, 0, unroll=DMA_UNROLL)

    @pl.when((g == 0) & (f == 0))
    def _():
        for_rows(lambda j: h2_copy(idx_ref, base, j).start())

    @pl.when(f == 0)
    def _():
        for_rows(lambda j: h2_copy(idx_ref, base, j).wait())
        xe_ref[...] = stage_ref[...].astype(BF16)
        acc_ref[...] = jnp.zeros_like(acc_ref)

    for jj in range(per):
        j = f * per + jj
        h2_copy(idxn_ref, base_n, j).start()
        x2_copy(j).start()

    xe = xe_ref[...]
    gg = jnp.dot(xe, wg_ref[0].astype(BF16), preferred_element_type=F32)
    uu = jnp.dot(xe, wu_ref[0].astype(BF16), preferred_element_type=F32)
    hid = (gg * _sigmoid(gg) * uu).astype(BF16)
    acc_ref[...] += jnp.dot(hid, wd_ref[0].astype(BF16), preferred_element_type=F32)

    @pl.when(f == nf - 1)
    def _():
        for_rows(lambda j: x2_copy(j).wait())
        for c0 in range(0, cap, LANES):
            gcol = jnp.broadcast_to(gate_ref[0, :, c0:c0 + LANES], (LANES, LANES)).T[:, 0:1]
            rows_ref[c0:c0 + LANES, :] += acc_ref[c0:c0 + LANES, :] * gcol
        for_rows(lambda j: x2_store(j).start())
        for_rows(lambda j: x2_store(j).wait())

    @pl.when((g == ngroups - 1) & (f == nf - 1))
    def _():
        for_rows(lambda j: h2_copy(idxn_ref, base_n, j).wait())


def _experts(idx, gate, h2, x1, w_gate, w_up, w_down, *, batch, tf=256):
    t, d = x1.shape
    ne, _, dff = w_gate.shape
    cap = idx.shape[-1]
    nf = dff // tf
    ngroups = ne * batch
    assert cap % nf == 0

    def cur(e, b, f):
        return (b * ne + e, 0, 0)

    def nxt(e, b, f):
        gn = jnp.minimum(e * batch + b + 1, ngroups - 1)
        return ((gn % batch) * ne + gn // batch, 0, 0)

    return pl.pallas_call(
        functools.partial(_experts_kernel, seq=t // batch, cap=cap, nf=nf, batch=batch, ngroups=ngroups),
        grid=(ne, batch, nf),
        in_specs=[pl.BlockSpec((1, 1, cap), cur, memory_space=pltpu.SMEM),
                  pl.BlockSpec((1, 1, cap), nxt, memory_space=pltpu.SMEM),
                  pl.BlockSpec((1, 1, cap), cur),
                  pl.BlockSpec(memory_space=pl.ANY), pl.BlockSpec(memory_space=pl.ANY),
                  pl.BlockSpec((1, d, tf), lambda e, b, f: (e, 0, f)),
                  pl.BlockSpec((1, d, tf), lambda e, b, f: (e, 0, f)),
                  pl.BlockSpec((1, tf, d), lambda e, b, f: (e, f, 0))],
        out_specs=pl.BlockSpec(memory_space=pl.ANY),
        out_shape=jax.ShapeDtypeStruct((t, d), F32),
        input_output_aliases={4: 0},
        scratch_shapes=[pltpu.VMEM((cap, d), F32), pltpu.VMEM((cap, d), F32), pltpu.VMEM((cap, d), BF16),
                        pltpu.VMEM((cap, d), F32), pltpu.SemaphoreType.DMA, pltpu.SemaphoreType.DMA,
                        pltpu.SemaphoreType.DMA],
        compiler_params=_params(("arbitrary",) * 3),
        name="experts",
    )(idx, idx, gate, h2, x1, w_gate, w_up, w_down)


def _ple_kernel(x_ref, p_ref, g_ref, wg_ref, wp_ref, o_ref):
    x = x_ref[...]
    hn = _rms(x, g_ref[...]).astype(BF16)
    gate = _sigmoid(jnp.dot(hn, wg_ref[...], preferred_element_type=F32))
    proj = jnp.dot(p_ref[...].astype(BF16), wp_ref[...], preferred_element_type=F32)
    o_ref[...] = x + gate * proj


def _ple(x2d, p2d, g, wg_bf, wp_bf, *, tm=512):
    t, d = x2d.shape
    q = p2d.shape[1]
    row = lambda i: (i, 0)
    return pl.pallas_call(
        _ple_kernel,
        grid=(t // tm,),
        in_specs=[pl.BlockSpec((tm, d), row), pl.BlockSpec((tm, q), row), _resident((1, d)),
                  _resident(wg_bf.shape), _resident(wp_bf.shape)],
        out_specs=pl.BlockSpec((tm, d), row),
        out_shape=jax.ShapeDtypeStruct((t, d), F32),
        compiler_params=_params(("arbitrary",)),
        name="ple",
    )(x2d, p2d, g.reshape(1, d), wg_bf, wp_bf)


def kernel(x, p, rel_bias, norm_mix, w_in, q_norm, k_norm, conv_w, conv_b, conv_ln_g, conv_ln_b,
           out_norm_attn, out_norm_conv, w_out, norm_ffn, w_router, w_gate, w_up, w_down,
           norm_ple, w_ple_gate, w_ple_proj):
    b, s, d = x.shape
    t = b * s
    ne = w_router.shape[-1]
    cap = EC_CAPACITY * s // ne
    x2d = x.reshape(t, d)
    for i in range(w_in.shape[0]):
        q, k, v, u = _in_proj(x2d, norm_mix[i], w_in[i].astype(BF16), q_norm[i], k_norm[i])
        a = q.shape[1]
        o_att = _attention(q.reshape(b, s, a), k.reshape(b, s, a), v.reshape(b, s, a), rel_bias)
        c_n = _conv_mixer(u.reshape(b, s, -1), conv_w[i], conv_b[i], conv_ln_g[i], conv_ln_b[i], out_norm_conv[i])
        x1, h2, aff = _out_proj(o_att.reshape(t, a), c_n.reshape(t, -1), x2d, out_norm_attn[i],
                                w_out[i].astype(BF16), norm_ffn[i], w_router[i])
        idx, gate = _route(aff, b, cap)
        x2 = _experts(idx, gate, h2, x1, w_gate[i], w_up[i], w_down[i], batch=b)
        x2d = _ple(x2, p[i].reshape(t, -1), norm_ple[i], w_ple_gate[i].astype(BF16), w_ple_proj[i].astype(BF16))
    return x2d.reshape(b, s, d)
```

```python
import functools
import math

import jax
import jax.numpy as jnp
from jax import lax
from jax.experimental import pallas as pl
from jax.experimental.pallas import tpu as pltpu

F32 = jnp.float32
BF16 = jnp.bfloat16

ATT_HEADS = 8
HEAD_DIM = 128
DW_CONV_SIZE = 31
DILATED_PAIRS = ((128, 1), (512, 4), (2048, 16))
N_BUCKETS = 32
MAX_DISTANCE = 1024
EC_CAPACITY = 2
EPS = 1e-6
NEG = -1e30

LANES = 128
VMEM_LIMIT = 56 * 1024 * 1024


def _params(sem, vmem=VMEM_LIMIT):
    return pltpu.CompilerParams(dimension_semantics=sem, vmem_limit_bytes=vmem)


def _resident(shape):
    nd = len(shape)
    return pl.BlockSpec(shape, lambda *_: (0,) * nd, pipeline_mode=pl.Buffered(1))


def _rms(x, g):
    return x * lax.rsqrt(jnp.mean(x * x, axis=-1, keepdims=True) + EPS) * g


def _sigmoid(x):
    return 1.0 / (1.0 + jnp.exp(-x))


def _split_bf16(x):
    hi = x.astype(BF16)
    return hi, (x - hi.astype(F32)).astype(BF16)


def _in_proj_kernel(x_ref, g_ref, w_ref, qn_ref, kn_ref, q_ref, k_ref, v_ref, u_ref, *, width):
    h = _rms(x_ref[...], g_ref[...]).astype(BF16)

    def proj(c):
        return jnp.dot(h, w_ref[:, c * width:(c + 1) * width], preferred_element_type=F32)

    def head_norm(z, gn, o_ref):
        for hh in range(width // HEAD_DIM):
            sl = slice(hh * HEAD_DIM, (hh + 1) * HEAD_DIM)
            o_ref[:, sl] = _rms(z[:, sl], gn).astype(o_ref.dtype)

    head_norm(proj(0), qn_ref[...], q_ref)
    head_norm(proj(1), kn_ref[...], k_ref)
    v_ref[...] = proj(2).astype(v_ref.dtype)
    u_ref[...] = proj(3) * _sigmoid(proj(4))


def _in_proj(x2d, norm_mix, w_in_bf, q_norm, k_norm, *, tm=512):
    t, d = x2d.shape
    width = w_in_bf.shape[1] // 5
    row = lambda i: (i, 0)
    outs = pl.pallas_call(
        functools.partial(_in_proj_kernel, width=width),
        grid=(t // tm,),
        in_specs=[pl.BlockSpec((tm, d), row), _resident((1, d)), _resident(w_in_bf.shape),
                  _resident((1, HEAD_DIM)), _resident((1, HEAD_DIM))],
        out_specs=[pl.BlockSpec((tm, width), row)] * 4,
        out_shape=[jax.ShapeDtypeStruct((t, width), BF16)] * 3 + [jax.ShapeDtypeStruct((t, width), F32)],
        compiler_params=_params(("arbitrary",)),
        name="in_proj",
    )(x2d, norm_mix.reshape(1, d), w_in_bf, q_norm.reshape(1, -1), k_norm.reshape(1, -1))
    return outs


ATT_REACH = max(w // 2 for w, _ in DILATED_PAIRS)


def _t5_bucket(rel):
    half_b = N_BUCKETS // 2
    exact = half_b // 2
    n = jnp.abs(rel)
    nf = jnp.maximum(n, 1).astype(F32)
    large = exact + (jnp.log(nf / exact) / math.log(MAX_DISTANCE / exact) * (half_b - exact)).astype(jnp.int32)
    large = jnp.minimum(large, half_b - 1)
    return jnp.where(rel > 0, half_b, 0) + jnp.where(n < exact, n, large)


def _attn_table(rel_bias, tq):
    nside = ATT_REACH // LANES + 1
    nd = 2 * nside + tq // LANES
    ncol = nd * LANES
    m = ncol + tq - 1
    d = jnp.arange(m, dtype=jnp.int32) - (nside * LANES + tq - 1)
    mult = jnp.zeros(d.shape, F32)
    for window, dil in DILATED_PAIRS:
        mult = mult + ((d % dil == 0) & (jnp.abs(d) <= window // 2)).astype(F32)
    bias = rel_bias[_t5_bucket(d)].astype(F32).T
    val = jnp.where(mult > 0, bias + jnp.log(jnp.maximum(mult, 1.0)), NEG)
    heads = val.shape[0]
    skew = jnp.tile(jnp.pad(val, ((0, 0), (0, 1))), (1, tq))[:, :tq * m].reshape(heads, tq, m)
    tab = skew[:, :, tq - 1:tq - 1 + ncol].reshape(heads, tq, nd, LANES)
    return jnp.transpose(tab, (0, 2, 1, 3))


def _attn_kernel(q_ref, k_ref, v_ref, tab_ref, o_ref, *, tq, seq, heads):
    qt = tq // LANES
    nside = ATT_REACH // LANES
    nwin = 2 * nside + qt
    nblk = seq // LANES
    qb = pl.program_id(2) * qt
    start_blk = jnp.clip(qb - nside, 0, nblk - nwin)
    start = pl.multiple_of(start_blk * LANES, LANES)
    scale = HEAD_DIM ** -0.5
    for hh in range(heads):
        sl = slice(hh * HEAD_DIM, (hh + 1) * HEAD_DIM)
        q = q_ref[0, :, sl]
        kw = k_ref[0, pl.ds(start, nwin * LANES), sl]
        s = lax.dot_general(q, kw, (((1,), (1,)), ((), ())), preferred_element_type=F32) * scale
        ntab = tab_ref.shape[1]
        parts = []
        for jj in range(nwin):
            rows = []
            for rh in range(qt):
                ti = jnp.clip(start_blk + jj - qb - rh + nside + 1, 0, ntab - 1)
                rows.append(s[rh * LANES:(rh + 1) * LANES, jj * LANES:(jj + 1) * LANES] + tab_ref[hh, ti])
            parts.append(rows[0] if qt == 1 else jnp.concatenate(rows, axis=0))
        s = jnp.concatenate(parts, axis=-1)
        m = jnp.max(s, axis=-1, keepdims=True)
        p = jnp.exp(s - m)
        l = jnp.sum(p, axis=-1, keepdims=True)
        vw = v_ref[0, pl.ds(start, nwin * LANES), sl]
        o = jnp.dot(p.astype(BF16), vw, preferred_element_type=F32)
        o_ref[0, :, sl] = o / l


def _attention(q, k, v, rel_bias, *, tq=256, head_groups=2):
    b, s, a = q.shape
    heads = ATT_HEADS // head_groups
    gw = heads * HEAD_DIM
    tab = _attn_table(rel_bias, LANES)
    nt = tab.shape[1]
    kv_spec = pl.BlockSpec((1, s, gw), lambda bi, g, i: (bi, 0, g), pipeline_mode=pl.Buffered(1))
    return pl.pallas_call(
        functools.partial(_attn_kernel, tq=tq, seq=s, heads=heads),
        grid=(b, head_groups, s // tq),
        in_specs=[pl.BlockSpec((1, tq, gw), lambda bi, g, i: (bi, i, g)), kv_spec, kv_spec,
                  pl.BlockSpec((heads, nt, LANES, LANES), lambda bi, g, i: (g, 0, 0, 0),
                               pipeline_mode=pl.Buffered(1))],
        out_specs=pl.BlockSpec((1, tq, gw), lambda bi, g, i: (bi, i, g)),
        out_shape=jax.ShapeDtypeStruct((b, s, a), F32),
        compiler_params=_params(("arbitrary",) * 3),
        name="attn",
    )(q, k, v, tab)


CONV_HALO = 16
CONV_ROWS = 32
CONV_LANES = 256


def _conv_kernel(u_ref, up_ref, un_ref, cw_ref, cb_ref, lg_ref, lb_ref, og_ref, o_ref, ext_ref, y_ref,
                 *, tm, ntiles):
    i = pl.program_id(1)
    c = u_ref.shape[-1]
    pad = DW_CONV_SIZE // 2
    ext_ref[0:CONV_HALO, :] = jnp.where(i == 0, 0.0, up_ref[0])
    ext_ref[CONV_HALO:CONV_HALO + tm, :] = u_ref[0]
    ext_ref[CONV_HALO + tm:2 * CONV_HALO + tm, :] = jnp.where(i == ntiles - 1, 0.0, un_ref[0])
    for r0 in range(0, tm, CONV_ROWS):
        for c0 in range(0, c, CONV_LANES):
            cs = slice(c0, c0 + CONV_LANES)
            acc = jnp.zeros((CONV_ROWS, CONV_LANES), F32)
            for t in range(DW_CONV_SIZE):
                lo = CONV_HALO - pad + r0 + t
                acc = acc + ext_ref[lo:lo + CONV_ROWS, cs] * cw_ref[t:t + 1, cs]
            y_ref[r0:r0 + CONV_ROWS, cs] = acc
    y = y_ref[...] + cb_ref[...]
    mu = jnp.mean(y, axis=-1, keepdims=True)
    yc = y - mu
    var = jnp.mean(yc * yc, axis=-1, keepdims=True)
    yn = yc * lax.rsqrt(var + EPS) * lg_ref[...] + lb_ref[...]
    sw = yn * _sigmoid(yn)
    o_ref[0] = _rms(sw, og_ref[...]).astype(o_ref.dtype)


def _conv_mixer(u, conv_w, conv_b, ln_g, ln_b, out_g, *, tm=256):
    b, s, c = u.shape
    ntiles = s // tm
    hb = tm // CONV_HALO
    nh = s // CONV_HALO
    vec = lambda a: a.reshape(1, c)
    return pl.pallas_call(
        functools.partial(_conv_kernel, tm=tm, ntiles=ntiles),
        grid=(b, ntiles),
        in_specs=[pl.BlockSpec((1, tm, c), lambda bi, i: (bi, i, 0)),
                  pl.BlockSpec((1, CONV_HALO, c), lambda bi, i: (bi, jnp.maximum(i * hb - 1, 0), 0)),
                  pl.BlockSpec((1, CONV_HALO, c), lambda bi, i: (bi, jnp.minimum((i + 1) * hb, nh - 1), 0)),
                  _resident((DW_CONV_SIZE, c))] + [_resident((1, c))] * 4,
        out_specs=pl.BlockSpec((1, tm, c), lambda bi, i: (bi, i, 0)),
        out_shape=jax.ShapeDtypeStruct((b, s, c), BF16),
        scratch_shapes=[pltpu.VMEM((tm + 2 * CONV_HALO, c), F32), pltpu.VMEM((tm, c), F32)],
        compiler_params=_params(("arbitrary",) * 2),
        name="conv",
    )(u, u, u, conv_w, vec(conv_b), vec(ln_g), vec(ln_b), vec(out_g))


def _out_proj_kernel(oa_ref, cn_ref, x_ref, ga_ref, wo_ref, gf_ref, wr_ref, x1_ref, h2_ref, aff_ref):
    a = oa_ref.shape[-1]
    an = _rms(oa_ref[...], ga_ref[...]).astype(BF16)
    acc = jnp.dot(an, wo_ref[0:a, :], preferred_element_type=F32)
    acc = acc + jnp.dot(cn_ref[...], wo_ref[a:, :], preferred_element_type=F32)
    x1 = x_ref[...] + acc
    x1_ref[...] = x1
    h2 = _rms(x1, gf_ref[...])
    h2_ref[...] = h2
    ne = wr_ref.shape[-1]
    h_hi, h_lo = _split_bf16(h2)
    w_hi, w_lo = _split_bf16(wr_ref[...])
    t = jnp.dot(h_hi, jnp.concatenate([w_hi, w_lo], axis=1), preferred_element_type=F32)
    logits = t[:, :ne] + (t[:, ne:] + jnp.dot(h_lo, w_hi, preferred_element_type=F32))
    e = jnp.exp(logits - jnp.max(logits, axis=-1, keepdims=True))
    aff_ref[...] = e / jnp.sum(e, axis=-1, keepdims=True)


def _out_proj(o_att, c_n, x2d, g_att, w_out_bf, g_ffn, w_router, *, tm=512):
    t, d = x2d.shape
    a = o_att.shape[1]
    ne = w_router.shape[1]
    row = lambda i: (i, 0)
    return pl.pallas_call(
        _out_proj_kernel,
        grid=(t // tm,),
        in_specs=[pl.BlockSpec((tm, a), row), pl.BlockSpec((tm, c_n.shape[1]), row), pl.BlockSpec((tm, d), row),
                  _resident((1, a)), _resident(w_out_bf.shape), _resident((1, d)), _resident(w_router.shape)],
        out_specs=[pl.BlockSpec((tm, d), row), pl.BlockSpec((tm, d), row), pl.BlockSpec((tm, ne), row)],
        out_shape=[jax.ShapeDtypeStruct((t, d), F32), jax.ShapeDtypeStruct((t, d), F32),
                   jax.ShapeDtypeStruct((t, ne), F32)],
        compiler_params=_params(("arbitrary",)),
        name="out_proj",
    )(o_att, c_n, x2d, g_att.reshape(1, a), w_out_bf, g_ffn.reshape(1, d), w_router)


ROUTE_BLK = 128


def _route_kernel(aff_ref, pos_ref, cnt_ref, m_ref, *, cap):
    s, ne = aff_ref.shape
    capf = float(cap)

    def keys():
        return pltpu.bitcast(aff_ref[...], jnp.int32)

    def count(mask):
        return jnp.sum(mask.astype(F32), axis=0, keepdims=True)

    def key_bit(it, thr):
        cand = thr | jnp.left_shift(jnp.int32(1), 30 - it)
        return jnp.where(count(keys() >= cand) >= capf, cand, thr)

    thr = lax.fori_loop(0, 31, key_bit, jnp.zeros((1, ne), jnp.int32))
    need = capf - count(keys() > thr)
    tok = lax.broadcasted_iota(jnp.int32, (s, ne), 0)

    tok_bits = (s - 1).bit_length()

    def tok_bit(it, lim):
        cand = lim | jnp.left_shift(jnp.int32(1), tok_bits - 1 - it)
        below = count((keys() == thr) & (tok < cand))
        return jnp.where(below < need, cand, lim)

    lim = lax.fori_loop(0, tok_bits, tok_bit, jnp.zeros((1, ne), jnp.int32))
    k = keys()
    m_ref[...] = ((k > thr) | ((k == thr) & (tok <= lim))).astype(F32)

    r = lax.broadcasted_iota(jnp.int32, (ROUTE_BLK, ROUTE_BLK), 0)
    cidx = lax.broadcasted_iota(jnp.int32, (ROUTE_BLK, ROUTE_BLK), 1)
    tri = (r > cidx).astype(BF16)

    def prefix(kb, carry):
        r0 = pl.multiple_of(kb * ROUTE_BLK, ROUTE_BLK)
        mb = m_ref[pl.ds(r0, ROUTE_BLK), :]
        local = jnp.dot(tri, mb.astype(BF16), preferred_element_type=F32)
        pos_ref[pl.ds(r0, ROUTE_BLK), :] = jnp.where(mb > 0, local + carry, -1.0)
        cnt_ref[pl.ds(kb, 1), :] = carry.astype(jnp.int32)
        return carry + jnp.sum(mb, axis=0, keepdims=True)

    lax.fori_loop(0, s // ROUTE_BLK, prefix, jnp.zeros((1, ne), F32))


def _route_lists_kernel(cnt_ref, pos_ref, aff_ref, idx_ref, gate_ref, ai_ref, ag_ref, *, cap):
    bi = pl.program_id(0)
    s, ne = pos_ref.shape
    nblk = s // ROUTE_BLK
    nsb = cap // LANES
    ai_ref[...] = jnp.zeros_like(ai_ref)
    ag_ref[...] = jnp.zeros_like(ag_ref)
    lane = lax.broadcasted_iota(jnp.int32, (ROUTE_BLK, LANES), 1).astype(F32)
    sub = lax.broadcasted_iota(jnp.int32, (ROUTE_BLK, LANES), 0).astype(F32)

    def fold(x):
        return jnp.sum(x.reshape(ROUTE_BLK // 8, 8, LANES), axis=0)

    for e in range(ne):
        def blk(kb, carry):
            r0 = pl.multiple_of(kb * ROUTE_BLK, ROUTE_BLK)
            before = cnt_ref[(bi * nblk + kb) * ne + e]
            grp = jnp.minimum(lax.shift_right_logical(before, LANES.bit_length() - 1), nsb - 1)
            rel = jnp.broadcast_to(pos_ref[pl.ds(r0, ROUTE_BLK), e:e + 1], (ROUTE_BLK, LANES)) \
                - jnp.asarray(grp * LANES, F32)
            aff_b = jnp.broadcast_to(aff_ref[pl.ds(r0, ROUTE_BLK), e:e + 1], (ROUTE_BLK, LANES))
            tokf = sub + jnp.asarray(kb * ROUTE_BLK, F32)
            row = e * (nsb + 1) + grp
            for half in range(2):
                hit = rel == lane + float(half * LANES)
                ai_ref[row + half] += fold(jnp.where(hit, tokf, 0.0))
                ag_ref[row + half] += fold(jnp.where(hit, aff_b, 0.0))
            return carry

        lax.fori_loop(0, nblk, blk, 0)
    for e in range(ne):
        for sb in range(nsb):
            ls = slice(sb * LANES, (sb + 1) * LANES)
            row = e * (nsb + 1) + sb
            idx_ref[e, :, ls] = jnp.sum(ai_ref[row], axis=0, keepdims=True).astype(jnp.int32)
            gate_ref[e, :, ls] = jnp.sum(ag_ref[row], axis=0, keepdims=True)


def _route(aff, batch, cap):
    t, ne = aff.shape
    s = t // batch
    nblk = s // ROUTE_BLK
    pos, cnt = pl.pallas_call(
        functools.partial(_route_kernel, cap=cap),
        grid=(batch,),
        in_specs=[pl.BlockSpec((s, ne), lambda bi: (bi, 0))],
        out_specs=[pl.BlockSpec((s, ne), lambda bi: (bi, 0)), pl.BlockSpec((nblk, ne), lambda bi: (bi, 0))],
        out_shape=[jax.ShapeDtypeStruct((t, ne), F32), jax.ShapeDtypeStruct((batch * nblk, ne), jnp.int32)],
        scratch_shapes=[pltpu.VMEM((s, ne), F32)],
        compiler_params=_params(("arbitrary",)),
        name="route_select",
    )(aff)
    nrows = ne * (cap // LANES + 1)
    return pl.pallas_call(
        functools.partial(_route_lists_kernel, cap=cap),
        grid_spec=pltpu.PrefetchScalarGridSpec(
            num_scalar_prefetch=1,
            grid=(batch,),
            in_specs=[pl.BlockSpec((s, ne), lambda bi, c: (bi, 0)), pl.BlockSpec((s, ne), lambda bi, c: (bi, 0))],
            out_specs=[pl.BlockSpec((ne, 1, cap), lambda bi, c: (bi, 0, 0))] * 2,
            scratch_shapes=[pltpu.VMEM((nrows, 8, LANES), F32), pltpu.VMEM((nrows, 8, LANES), F32)]),
        out_shape=[jax.ShapeDtypeStruct((batch * ne, 1, cap), jnp.int32),
                   jax.ShapeDtypeStruct((batch * ne, 1, cap), F32)],
        compiler_params=_params(("arbitrary",)),
        name="route_lists",
    )(cnt.reshape(-1), pos, aff)


DMA_UNROLL = 8


def _experts_kernel(idx_ref, idxn_ref, gate_ref, h2_hbm, x1_hbm, wg_ref, wu_ref, wd_ref, x2_hbm,
                    stage_ref, rows_ref, xe_ref, acc_ref, sem_h, sem_r, sem_w, *, seq, cap, nf, batch, ngroups):
    b = pl.program_id(1)
    f = pl.program_id(2)
    g = pl.program_id(0) * batch + b
    base = b * seq
    base_n = lax.rem(b + 1, batch) * seq
    per = cap // nf

    def h2_copy(iref, row_base, j):
        return pltpu.make_async_copy(h2_hbm.at[pl.ds(row_base + iref[0, 0, j], 1), :],
                                     stage_ref.at[pl.ds(j, 1), :], sem_h)

    def x2_copy(j):
        return pltpu.make_async_copy(x2_hbm.at[pl.ds(base + idx_ref[0, 0, j], 1), :],
                                     rows_ref.at[pl.ds(j, 1), :], sem_r)

    def x2_store(j):
        return pltpu.make_async_copy(rows_ref.at[pl.ds(j, 1), :],
                                     x2_hbm.at[pl.ds(base + idx_ref[0, 0, j], 1), :], sem_w)

    def for_rows(fn):
        def body(j, carry):
            fn(j)
            return carry
        lax.fori_loop(0, cap, body, 0, unroll=DMA_UNROLL)

    @pl.when((g == 0) & (f == 0))
    def _():
        for_rows(lambda j: h2_copy(idx_ref, base, j).start())

    @pl.when(f == 0)
    def _():
        for_rows(lambda j: h2_copy(idx_ref, base, j).wait())
        xe_ref[...] = stage_ref[...].astype(BF16)
        acc_ref[...] = jnp.zeros_like(acc_ref)

    for jj in range(per):
        j = f * per + jj
        h2_copy(idxn_ref, base_n, j).start()
        x2_copy(j).start()

    xe = xe_ref[...]
    gg = jnp.dot(xe, wg_ref[0].astype(BF16), preferred_element_type=F32)
    uu = jnp.dot(xe, wu_ref[0].astype(BF16), preferred_element_type=F32)
    hid = (gg * _sigmoid(gg) * uu).astype(BF16)
    acc_ref[...] += jnp.dot(hid, wd_ref[0].astype(BF16), preferred_element_type=F32)

    @pl.when(f == nf - 1)
    def _():
        for_rows(lambda j: x2_copy(j).wait())
        for c0 in range(0, cap, LANES):
            gcol = jnp.broadcast_to(gate_ref[0, :, c0:c0 + LANES], (LANES, LANES)).T[:, 0:1]
            rows_ref[c0:c0 + LANES, :] += acc_ref[c0:c0 + LANES, :] * gcol
        for_rows(lambda j: x2_store(j).start())
        for_rows(lambda j: x2_store(j).wait())

    @pl.when((g == ngroups - 1) & (f == nf - 1))
    def _():
        for_rows(lambda j: h2_copy(idxn_ref, base_n, j).wait())


def _experts(idx, gate, h2, x1, w_gate, w_up, w_down, *, batch, tf=256):
    t, d = x1.shape
    ne, _, dff = w_gate.shape
    cap = idx.shape[-1]
    nf = dff // tf
    ngroups = ne * batch
    assert cap % nf == 0

    def cur(e, b, f):
        return (b * ne + e, 0, 0)

    def nxt(e, b, f):
        gn = jnp.minimum(e * batch + b + 1, ngroups - 1)
        return ((gn % batch) * ne + gn // batch, 0, 0)

    return pl.pallas_call(
        functools.partial(_experts_kernel, seq=t // batch, cap=cap, nf=nf, batch=batch, ngroups=ngroups),
        grid=(ne, batch, nf),
        in_specs=[pl.BlockSpec((1, 1, cap), cur, memory_space=pltpu.SMEM),
                  pl.BlockSpec((1, 1, cap), nxt, memory_space=pltpu.SMEM),
                  pl.BlockSpec((1, 1, cap), cur),
                  pl.BlockSpec(memory_space=pl.ANY), pl.BlockSpec(memory_space=pl.ANY),
                  pl.BlockSpec((1, d, tf), lambda e, b, f: (e, 0, f)),
                  pl.BlockSpec((1, d, tf), lambda e, b, f: (e, 0, f)),
                  pl.BlockSpec((1, tf, d), lambda e, b, f: (e, f, 0))],
        out_specs=pl.BlockSpec(memory_space=pl.ANY),
        out_shape=jax.ShapeDtypeStruct((t, d), F32),
        input_output_aliases={4: 0},
        scratch_shapes=[pltpu.VMEM((cap, d), F32), pltpu.VMEM((cap, d), F32), pltpu.VMEM((cap, d), BF16),
                        pltpu.VMEM((cap, d), F32), pltpu.SemaphoreType.DMA, pltpu.SemaphoreType.DMA,
                        pltpu.SemaphoreType.DMA],
        compiler_params=_params(("arbitrary",) * 3),
        name="experts",
    )(idx, idx, gate, h2, x1, w_gate, w_up, w_down)


def _ple_kernel(x_ref, p_ref, g_ref, wg_ref, wp_ref, o_ref):
    x = x_ref[...]
    hn = _rms(x, g_ref[...]).astype(BF16)
    gate = _sigmoid(jnp.dot(hn, wg_ref[...], preferred_element_type=F32))
    proj = jnp.dot(p_ref[...].astype(BF16), wp_ref[...], preferred_element_type=F32)
    o_ref[...] = x + gate * proj


def _ple(x2d, p2d, g, wg_bf, wp_bf, *, tm=512):
    t, d = x2d.shape
    q = p2d.shape[1]
    row = lambda i: (i, 0)
    return pl.pallas_call(
        _ple_kernel,
        grid=(t // tm,),
        in_specs=[pl.BlockSpec((tm, d), row), pl.BlockSpec((tm, q), row), _resident((1, d)),
                  _resident(wg_bf.shape), _resident(wp_bf.shape)],
        out_specs=pl.BlockSpec((tm, d), row),
        out_shape=jax.ShapeDtypeStruct((t, d), F32),
        compiler_params=_params(("arbitrary",)),
        name="ple",
    )(x2d, p2d, g.reshape(1, d), wg_bf, wp_bf)


def kernel(x, p, rel_bias, norm_mix, w_in, q_norm, k_norm, conv_w, conv_b, conv_ln_g, conv_ln_b,
           out_norm_attn, out_norm_conv, w_out, norm_ffn, w_router, w_gate, w_up, w_down,
           norm_ple, w_ple_gate, w_ple_proj):
    b, s, d = x.shape
    t = b * s
    ne = w_router.shape[-1]
    cap = EC_CAPACITY * s // ne
    x2d = x.reshape(t, d)
    for i in range(w_in.shape[0]):
        q, k, v, u = _in_proj(x2d, norm_mix[i], w_in[i].astype(BF16), q_norm[i], k_norm[i])
        a = q.shape[1]
        o_att = _attention(q.reshape(b, s, a), k.reshape(b, s, a), v.reshape(b, s, a), rel_bias)
        c_n = _conv_mixer(u.reshape(b, s, -1), conv_w[i], conv_b[i], conv_ln_g[i], conv_ln_b[i], out_norm_conv[i])
        x1, h2, aff = _out_proj(o_att.reshape(t, a), c_n.reshape(t, -1), x2d, out_norm_attn[i],
                                w_out[i].astype(BF16), norm_ffn[i], w_router[i])
        idx, gate = _route(aff, b, cap)
        x2 = _experts(idx, gate, h2, x1, w_gate[i], w_up[i], w_down[i], batch=b)
        x2d = _ple(x2, p[i].reshape(t, -1), norm_ple[i], w_ple_gate[i].astype(BF16), w_ple_proj[i].astype(BF16))
    return x2d.reshape(b, s, d)
```

```python
import functools
import math

import jax
import jax.numpy as jnp
from jax import lax
from jax.experimental import pallas as pl
from jax.experimental.pallas import tpu as pltpu

F32 = jnp.float32
BF16 = jnp.bfloat16

ATT_HEADS = 8
HEAD_DIM = 128
DW_CONV_SIZE = 31
DILATED_PAIRS = ((128, 1), (512, 4), (2048, 16))
N_BUCKETS = 32
MAX_DISTANCE = 1024
EC_CAPACITY = 2
EPS = 1e-6
NEG = -1e30

LANES = 128
SUBLANES = 8
VMEM_LIMIT = 56 * 1024 * 1024


def _params(sem, vmem=VMEM_LIMIT):
    return pltpu.CompilerParams(dimension_semantics=sem, vmem_limit_bytes=vmem)


def _resident(shape):
    nd = len(shape)
    return pl.BlockSpec(shape, lambda *_: (0,) * nd, pipeline_mode=pl.Buffered(1))


def _rms(x, g):
    return x * lax.rsqrt(jnp.mean(x * x, axis=-1, keepdims=True) + EPS) * g


def _sigmoid(x):
    return 1.0 / (1.0 + jnp.exp(-x))


def _split_bf16(x):
    hi = x.astype(BF16)
    return hi, (x - hi.astype(F32)).astype(BF16)


def _in_proj_kernel(x_ref, g_ref, w_ref, qn_ref, kn_ref, q_ref, k_ref, v_ref, u_ref, *, width):
    h = _rms(x_ref[...], g_ref[...]).astype(BF16)

    def proj(c):
        return jnp.dot(h, w_ref[:, c * width:(c + 1) * width], preferred_element_type=F32)

    def head_norm(z, gn, o_ref):
        for hh in range(width // HEAD_DIM):
            sl = slice(hh * HEAD_DIM, (hh + 1) * HEAD_DIM)
            o_ref[:, sl] = _rms(z[:, sl], gn).astype(o_ref.dtype)

    head_norm(proj(0), qn_ref[...], q_ref)
    head_norm(proj(1), kn_ref[...], k_ref)
    v_ref[...] = proj(2).astype(v_ref.dtype)
    u_ref[...] = proj(3) * _sigmoid(proj(4))


def _in_proj(x2d, norm_mix, w_in_bf, q_norm, k_norm, *, tm=512):
    t, d = x2d.shape
    width = w_in_bf.shape[1] // 5
    row = lambda i: (i, 0)
    outs = pl.pallas_call(
        functools.partial(_in_proj_kernel, width=width),
        grid=(t // tm,),
        in_specs=[pl.BlockSpec((tm, d), row), _resident((1, d)), _resident(w_in_bf.shape),
                  _resident((1, HEAD_DIM)), _resident((1, HEAD_DIM))],
        out_specs=[pl.BlockSpec((tm, width), row)] * 4,
        out_shape=[jax.ShapeDtypeStruct((t, width), BF16)] * 3 + [jax.ShapeDtypeStruct((t, width), F32)],
        compiler_params=_params(("arbitrary",)),
        name="in_proj",
    )(x2d, norm_mix.reshape(1, d), w_in_bf, q_norm.reshape(1, -1), k_norm.reshape(1, -1))
    return outs


ATT_REACH = max(w // 2 for w, _ in DILATED_PAIRS)


def _t5_bucket(rel):
    half_b = N_BUCKETS // 2
    exact = half_b // 2
    n = jnp.abs(rel)
    nf = jnp.maximum(n, 1).astype(F32)
    large = exact + (jnp.log(nf / exact) / math.log(MAX_DISTANCE / exact) * (half_b - exact)).astype(jnp.int32)
    large = jnp.minimum(large, half_b - 1)
    return jnp.where(rel > 0, half_b, 0) + jnp.where(n < exact, n, large)


def _attn_table(rel_bias, tq):
    nside = ATT_REACH // LANES + 1
    nd = 2 * nside + tq // LANES
    ncol = nd * LANES
    m = ncol + tq - 1
    d = jnp.arange(m, dtype=jnp.int32) - (nside * LANES + tq - 1)
    mult = jnp.zeros(d.shape, F32)
    for window, dil in DILATED_PAIRS:
        mult = mult + ((d % dil == 0) & (jnp.abs(d) <= window // 2)).astype(F32)
    bias = rel_bias[_t5_bucket(d)].astype(F32).T
    val = jnp.where(mult > 0, bias + jnp.log(jnp.maximum(mult, 1.0)), NEG)
    heads = val.shape[0]
    skew = jnp.tile(jnp.pad(val, ((0, 0), (0, 1))), (1, tq))[:, :tq * m].reshape(heads, tq, m)
    tab = skew[:, :, tq - 1:tq - 1 + ncol].reshape(heads, tq, nd, LANES)
    return jnp.transpose(tab, (0, 2, 1, 3))


def _attn_kernel(q_ref, k_ref, v_ref, tab_ref, o_ref, *, tq, seq, heads):
    qt = tq // LANES
    nside = ATT_REACH // LANES
    nwin = 2 * nside + qt
    nblk = seq // LANES
    qb = pl.program_id(2) * qt
    start_blk = jnp.clip(qb - nside, 0, nblk - nwin)
    start = pl.multiple_of(start_blk * LANES, LANES)
    scale = HEAD_DIM ** -0.5
    for hh in range(heads):
        sl = slice(hh * HEAD_DIM, (hh + 1) * HEAD_DIM)
        q = q_ref[0, :, sl]
        kw = k_ref[0, pl.ds(start, nwin * LANES), sl]
        s = lax.dot_general(q, kw, (((1,), (1,)), ((), ())), preferred_element_type=F32) * scale
        ntab = tab_ref.shape[1]
        parts = []
        for jj in range(nwin):
            rows = []
            for rh in range(qt):
                ti = jnp.clip(start_blk + jj - qb - rh + nside + 1, 0, ntab - 1)
                rows.append(s[rh * LANES:(rh + 1) * LANES, jj * LANES:(jj + 1) * LANES] + tab_ref[hh, ti])
            parts.append(rows[0] if qt == 1 else jnp.concatenate(rows, axis=0))
        s = jnp.concatenate(parts, axis=-1)
        m = jnp.max(s, axis=-1, keepdims=True)
        p = jnp.exp(s - m)
        l = jnp.sum(p, axis=-1, keepdims=True)
        vw = v_ref[0, pl.ds(start, nwin * LANES), sl]
        o = jnp.dot(p.astype(BF16), vw, preferred_element_type=F32)
        o_ref[0, :, sl] = o / l


def _attention(q, k, v, rel_bias, *, tq=256, head_groups=2):
    b, s, a = q.shape
    heads = ATT_HEADS // head_groups
    gw = heads * HEAD_DIM
    tab = _attn_table(rel_bias, LANES)
    nt = tab.shape[1]
    kv_spec = pl.BlockSpec((1, s, gw), lambda bi, g, i: (bi, 0, g), pipeline_mode=pl.Buffered(1))
    return pl.pallas_call(
        functools.partial(_attn_kernel, tq=tq, seq=s, heads=heads),
        grid=(b, head_groups, s // tq),
        in_specs=[pl.BlockSpec((1, tq, gw), lambda bi, g, i: (bi, i, g)), kv_spec, kv_spec,
                  pl.BlockSpec((heads, nt, LANES, LANES), lambda bi, g, i: (g, 0, 0, 0),
                               pipeline_mode=pl.Buffered(1))],
        out_specs=pl.BlockSpec((1, tq, gw), lambda bi, g, i: (bi, i, g)),
        out_shape=jax.ShapeDtypeStruct((b, s, a), F32),
        compiler_params=_params(("arbitrary",) * 3),
        name="attn",
    )(q, k, v, tab)


CONV_HALO = 16
CONV_ROWS = 32
CONV_LANES = 256


def _conv_kernel(u_ref, up_ref, un_ref, cw_ref, cb_ref, lg_ref, lb_ref, og_ref, o_ref, ext_ref, y_ref, ph_ref,
                 *, tm, ntiles):
    i = pl.program_id(1)
    c = u_ref.shape[-1]
    pad = DW_CONV_SIZE // 2
    ext_ref[0:CONV_HALO, :] = jnp.where(i == 0, 0.0, up_ref[0])
    ext_ref[CONV_HALO:CONV_HALO + tm, :] = u_ref[0]
    ext_ref[CONV_HALO + tm:2 * CONV_HALO + tm, :] = jnp.where(i == ntiles - 1, 0.0, un_ref[0])
    span = tm + 2 * CONV_HALO - SUBLANES
    for ph in range(1, SUBLANES):
        ph_ref[ph - 1] = ext_ref[ph:ph + span, :]
    for r0 in range(0, tm, CONV_ROWS):
        for c0 in range(0, c, CONV_LANES):
            cs = slice(c0, c0 + CONV_LANES)
            acc = jnp.zeros((CONV_ROWS, CONV_LANES), F32)
            for t in range(DW_CONV_SIZE):
                lo = CONV_HALO - pad + r0 + t
                ph, al = lo % SUBLANES, lo - lo % SUBLANES
                src = ext_ref if ph == 0 else ph_ref.at[ph - 1]
                acc = acc + src[al:al + CONV_ROWS, cs] * cw_ref[t:t + 1, cs]
            y_ref[r0:r0 + CONV_ROWS, cs] = acc
    y = y_ref[...] + cb_ref[...]
    mu = jnp.mean(y, axis=-1, keepdims=True)
    yc = y - mu
    var = jnp.mean(yc * yc, axis=-1, keepdims=True)
    yn = yc * lax.rsqrt(var + EPS) * lg_ref[...] + lb_ref[...]
    sw = yn * _sigmoid(yn)
    o_ref[0] = _rms(sw, og_ref[...]).astype(o_ref.dtype)


def _conv_mixer(u, conv_w, conv_b, ln_g, ln_b, out_g, *, tm=256):
    b, s, c = u.shape
    ntiles = s // tm
    hb = tm // CONV_HALO
    nh = s // CONV_HALO
    vec = lambda a: a.reshape(1, c)
    return pl.pallas_call(
        functools.partial(_conv_kernel, tm=tm, ntiles=ntiles),
        grid=(b, ntiles),
        in_specs=[pl.BlockSpec((1, tm, c), lambda bi, i: (bi, i, 0)),
                  pl.BlockSpec((1, CONV_HALO, c), lambda bi, i: (bi, jnp.maximum(i * hb - 1, 0), 0)),
                  pl.BlockSpec((1, CONV_HALO, c), lambda bi, i: (bi, jnp.minimum((i + 1) * hb, nh - 1), 0)),
                  _resident((DW_CONV_SIZE, c))] + [_resident((1, c))] * 4,
        out_specs=pl.BlockSpec((1, tm, c), lambda bi, i: (bi, i, 0)),
        out_shape=jax.ShapeDtypeStruct((b, s, c), BF16),
        scratch_shapes=[pltpu.VMEM((tm + 2 * CONV_HALO, c), F32), pltpu.VMEM((tm, c), F32),
                        pltpu.VMEM((SUBLANES - 1, tm + 2 * CONV_HALO - SUBLANES, c), F32)],
        compiler_params=_params(("arbitrary",) * 2),
        name="conv",
    )(u, u, u, conv_w, vec(conv_b), vec(ln_g), vec(ln_b), vec(out_g))


def _out_proj_kernel(oa_ref, cn_ref, x_ref, ga_ref, wo_ref, gf_ref, wr_ref, x1_ref, h2_ref, aff_ref):
    a = oa_ref.shape[-1]
    an = _rms(oa_ref[...], ga_ref[...]).astype(BF16)
    acc = jnp.dot(an, wo_ref[0:a, :], preferred_element_type=F32)
    acc = acc + jnp.dot(cn_ref[...], wo_ref[a:, :], preferred_element_type=F32)
    x1 = x_ref[...] + acc
    x1_ref[...] = x1
    h2 = _rms(x1, gf_ref[...])
    h2_ref[...] = h2
    ne = wr_ref.shape[-1]
    h_hi, h_lo = _split_bf16(h2)
    w_hi, w_lo = _split_bf16(wr_ref[...])
    t = jnp.dot(h_hi, jnp.concatenate([w_hi, w_lo], axis=1), preferred_element_type=F32)
    logits = t[:, :ne] + (t[:, ne:] + jnp.dot(h_lo, w_hi, preferred_element_type=F32))
    e = jnp.exp(logits - jnp.max(logits, axis=-1, keepdims=True))
    aff_ref[...] = e / jnp.sum(e, axis=-1, keepdims=True)


def _out_proj(o_att, c_n, x2d, g_att, w_out_bf, g_ffn, w_router, *, tm=512):
    t, d = x2d.shape
    a = o_att.shape[1]
    ne = w_router.shape[1]
    row = lambda i: (i, 0)
    return pl.pallas_call(
        _out_proj_kernel,
        grid=(t // tm,),
        in_specs=[pl.BlockSpec((tm, a), row), pl.BlockSpec((tm, c_n.shape[1]), row), pl.BlockSpec((tm, d), row),
                  _resident((1, a)), _resident(w_out_bf.shape), _resident((1, d)), _resident(w_router.shape)],
        out_specs=[pl.BlockSpec((tm, d), row), pl.BlockSpec((tm, d), row), pl.BlockSpec((tm, ne), row)],
        out_shape=[jax.ShapeDtypeStruct((t, d), F32), jax.ShapeDtypeStruct((t, d), F32),
                   jax.ShapeDtypeStruct((t, ne), F32)],
        compiler_params=_params(("arbitrary",)),
        name="out_proj",
    )(o_att, c_n, x2d, g_att.reshape(1, a), w_out_bf, g_ffn.reshape(1, d), w_router)


ROUTE_BLK = 128


def _route_kernel(aff_ref, pos_ref, cnt_ref, m_ref, *, cap):
    s, ne = aff_ref.shape
    capf = float(cap)

    def keys():
        return pltpu.bitcast(aff_ref[...], jnp.int32)

    def count(mask):
        return jnp.sum(mask.astype(F32), axis=0, keepdims=True)

    def key_bit(it, thr):
        cand = thr | jnp.left_shift(jnp.int32(1), 30 - it)
        return jnp.where(count(keys() >= cand) >= capf, cand, thr)

    thr = lax.fori_loop(0, 31, key_bit, jnp.zeros((1, ne), jnp.int32))
    need = capf - count(keys() > thr)
    tok = lax.broadcasted_iota(jnp.int32, (s, ne), 0)

    tok_bits = (s - 1).bit_length()

    def tok_bit(it, lim):
        cand = lim | jnp.left_shift(jnp.int32(1), tok_bits - 1 - it)
        below = count((keys() == thr) & (tok < cand))
        return jnp.where(below < need, cand, lim)

    lim = lax.fori_loop(0, tok_bits, tok_bit, jnp.zeros((1, ne), jnp.int32))
    k = keys()
    m_ref[...] = ((k > thr) | ((k == thr) & (tok <= lim))).astype(F32)

    r = lax.broadcasted_iota(jnp.int32, (ROUTE_BLK, ROUTE_BLK), 0)
    cidx = lax.broadcasted_iota(jnp.int32, (ROUTE_BLK, ROUTE_BLK), 1)
    tri = (r > cidx).astype(BF16)

    def prefix(kb, carry):
        r0 = pl.multiple_of(kb * ROUTE_BLK, ROUTE_BLK)
        mb = m_ref[pl.ds(r0, ROUTE_BLK), :]
        local = jnp.dot(tri, mb.astype(BF16), preferred_element_type=F32)
        pos_ref[pl.ds(r0, ROUTE_BLK), :] = jnp.where(mb > 0, local + carry, -1.0)
        cnt_ref[pl.ds(kb, 1), :] = carry.astype(jnp.int32)
        return carry + jnp.sum(mb, axis=0, keepdims=True)

    lax.fori_loop(0, s // ROUTE_BLK, prefix, jnp.zeros((1, ne), F32))


def _route_lists_kernel(cnt_ref, pos_ref, aff_ref, idx_ref, gate_ref, ai_ref, ag_ref, *, cap):
    bi = pl.program_id(0)
    s, ne = pos_ref.shape
    nblk = s // ROUTE_BLK
    nsb = cap // LANES
    ai_ref[...] = jnp.zeros_like(ai_ref)
    ag_ref[...] = jnp.zeros_like(ag_ref)
    lane = lax.broadcasted_iota(jnp.int32, (ROUTE_BLK, LANES), 1).astype(F32)
    sub = lax.broadcasted_iota(jnp.int32, (ROUTE_BLK, LANES), 0).astype(F32)

    def fold(x):
        return jnp.sum(x.reshape(ROUTE_BLK // 8, 8, LANES), axis=0)

    for e in range(ne):
        def blk(kb, carry):
            r0 = pl.multiple_of(kb * ROUTE_BLK, ROUTE_BLK)
            before = cnt_ref[(bi * nblk + kb) * ne + e]
            grp = jnp.minimum(lax.shift_right_logical(before, LANES.bit_length() - 1), nsb - 1)
            rel = jnp.broadcast_to(pos_ref[pl.ds(r0, ROUTE_BLK), e:e + 1], (ROUTE_BLK, LANES)) \
                - jnp.asarray(grp * LANES, F32)
            aff_b = jnp.broadcast_to(aff_ref[pl.ds(r0, ROUTE_BLK), e:e + 1], (ROUTE_BLK, LANES))
            tokf = sub + jnp.asarray(kb * ROUTE_BLK, F32)
            row = e * (nsb + 1) + grp
            for half in range(2):
                hit = rel == lane + float(half * LANES)
                ai_ref[row + half] += fold(jnp.where(hit, tokf, 0.0))
                ag_ref[row + half] += fold(jnp.where(hit, aff_b, 0.0))
            return carry

        lax.fori_loop(0, nblk, blk, 0)
    for e in range(ne):
        for sb in range(nsb):
            ls = slice(sb * LANES, (sb + 1) * LANES)
            row = e * (nsb + 1) + sb
            idx_ref[e, :, ls] = jnp.sum(ai_ref[row], axis=0, keepdims=True).astype(jnp.int32)
            gate_ref[e, :, ls] = jnp.sum(ag_ref[row], axis=0, keepdims=True)


def _route(aff, batch, cap):
    t, ne = aff.shape
    s = t // batch
    nblk = s // ROUTE_BLK
    pos, cnt = pl.pallas_call(
        functools.partial(_route_kernel, cap=cap),
        grid=(batch,),
        in_specs=[pl.BlockSpec((s, ne), lambda bi: (bi, 0))],
        out_specs=[pl.BlockSpec((s, ne), lambda bi: (bi, 0)), pl.BlockSpec((nblk, ne), lambda bi: (bi, 0))],
        out_shape=[jax.ShapeDtypeStruct((t, ne), F32), jax.ShapeDtypeStruct((batch * nblk, ne), jnp.int32)],
        scratch_shapes=[pltpu.VMEM((s, ne), F32)],
        compiler_params=_params(("arbitrary",)),
        name="route_select",
    )(aff)
    nrows = ne * (cap // LANES + 1)
    return pl.pallas_call(
        functools.partial(_route_lists_kernel, cap=cap),
        grid_spec=pltpu.PrefetchScalarGridSpec(
            num_scalar_prefetch=1,
            grid=(batch,),
            in_specs=[pl.BlockSpec((s, ne), lambda bi, c: (bi, 0)), pl.BlockSpec((s, ne), lambda bi, c: (bi, 0))],
            out_specs=[pl.BlockSpec((ne, 1, cap), lambda bi, c: (bi, 0, 0))] * 2,
            scratch_shapes=[pltpu.VMEM((nrows, 8, LANES), F32), pltpu.VMEM((nrows, 8, LANES), F32)]),
        out_shape=[jax.ShapeDtypeStruct((batch * ne, 1, cap), jnp.int32),
                   jax.ShapeDtypeStruct((batch * ne, 1, cap), F32)],
        compiler_params=_params(("arbitrary",)),
        name="route_lists",
    )(cnt.reshape(-1), pos, aff)


DMA_UNROLL = 8


def _experts_kernel(idx_ref, idxn_ref, gate_ref, h2_hbm, x1_hbm, wg_ref, wu_ref, wd_ref, x2_hbm,
                    stage_ref, rows_ref, xe_ref, acc_ref, sem_h, sem_r, sem_w, *, seq, cap, nf, batch, ngroups):
    b = pl.program_id(1)
    f = pl.program_id(2)
    g = pl.program_id(0) * batch + b
    base = b * seq
    base_n = lax.rem(b + 1, batch) * seq
    per = cap // nf

    def h2_copy(iref, row_base, j):
        return pltpu.make_async_copy(h2_hbm.at[pl.ds(row_base + iref[0, 0, j], 1), :],
                                     stage_ref.at[pl.ds(j, 1), :], sem_h)

    def x2_copy(j):
        return pltpu.make_async_copy(x2_hbm.at[pl.ds(base + idx_ref[0, 0, j], 1), :],
                                     rows_ref.at[pl.ds(j, 1), :], sem_r)

    def x2_store(j):
        return pltpu.make_async_copy(rows_ref.at[pl.ds(j, 1), :],
                                     x2_hbm.at[pl.ds(base + idx_ref[0, 0, j], 1), :], sem_w)

    def for_rows(fn):
        def body(j, carry):
            fn(j)
            return carry
        lax.fori_loop(0, cap, body, 0, unroll=DMA_UNROLL)

    @pl.when((g == 0) & (f == 0))
    def _():
        for_rows(lambda j: h2_copy(idx_ref, base, j).start())

    @pl.when(f == 0)
    def _():
        for_rows(lambda j: h2_copy(idx_ref, base, j).wait())
        xe_ref[...] = stage_ref[...].astype(BF16)
        acc_ref[...] = jnp.zeros_like(acc_ref)

    for jj in range(per):
        j = f * per + jj
        h2_copy(idxn_ref, base_n, j).start()
        x2_copy(j).start()

    xe = xe_ref[...]
    gg = jnp.dot(xe, wg_ref[0].astype(BF16), preferred_element_type=F32)
    uu = jnp.dot(xe, wu_ref[0].astype(BF16), preferred_element_type=F32)
    hid = (gg * _sigmoid(gg) * uu).astype(BF16)
    acc_ref[...] += jnp.dot(hid, wd_ref[0].astype(BF16), preferred_element_type=F32)

    @pl.when(f == nf - 1)
    def _():
        for_rows(lambda j: x2_copy(j).wait())
        for c0 in range(0, cap, LANES):
            gcol = jnp.broadcast_to(gate_ref[0, :, c0:c0 + LANES], (LANES, LANES)).T[:, 0:1]
            rows_ref[c0:c0 + LANES, :] += acc_ref[c0:c0 + LANES, :] * gcol
        for_rows(lambda j: x2_store(j).start())
        for_rows(lambda j: x2_store(j).wait())

    @pl.when((g == ngroups - 1) & (f == nf - 1))
    def _():
        for_rows(lambda j: h2_copy(idxn_ref, base_n, j).wait())


def _experts(idx, gate, h2, x1, w_gate, w_up, w_down, *, batch, tf=256):
    t, d = x1.shape
    ne, _, dff = w_gate.shape
    cap = idx.shape[-1]
    nf = dff // tf
    ngroups = ne * batch
    assert cap % nf == 0

    def cur(e, b, f):
        return (b * ne + e, 0, 0)

    def nxt(e, b, f):
        gn = jnp.minimum(e * batch + b + 1, ngroups - 1)
        return ((gn % batch) * ne + gn // batch, 0, 0)

    return pl.pallas_call(
        functools.partial(_experts_kernel, seq=t // batch, cap=cap, nf=nf, batch=batch, ngroups=ngroups),
        grid=(ne, batch, nf),
        in_specs=[pl.BlockSpec((1, 1, cap), cur, memory_space=pltpu.SMEM),
                  pl.BlockSpec((1, 1, cap), nxt, memory_space=pltpu.SMEM),
                  pl.BlockSpec((1, 1, cap), cur),
                  pl.BlockSpec(memory_space=pl.ANY), pl.BlockSpec(memory_space=pl.ANY),
                  pl.BlockSpec((1, d, tf), lambda e, b, f: (e, 0, f)),
                  pl.BlockSpec((1, d, tf), lambda e, b, f: (e, 0, f)),
                  pl.BlockSpec((1, tf, d), lambda e, b, f: (e, f, 0))],
        out_specs=pl.BlockSpec(memory_space=pl.ANY),
        out_shape=jax.ShapeDtypeStruct((t, d), F32),
        input_output_aliases={4: 0},
        scratch_shapes=[pltpu.VMEM((cap, d), F32), pltpu.VMEM((cap, d), F32), pltpu.VMEM((cap, d), BF16),
                        pltpu.VMEM((cap, d), F32), pltpu.SemaphoreType.DMA, pltpu.SemaphoreType.DMA,
                        pltpu.SemaphoreType.DMA],
        compiler_params=_params(("arbitrary",) * 3),
        name="experts",
    )(idx, idx, gate, h2, x1, w_gate, w_up, w_down)


def _ple_kernel(x_ref, p_ref, g_ref, wg_ref, wp_ref, o_ref):
    x = x_ref[...]
    hn = _rms(x, g_ref[...]).astype(BF16)
    gate = _sigmoid(jnp.dot(hn, wg_ref[...], preferred_element_type=F32))
    proj = jnp.dot(p_ref[...].astype(BF16), wp_ref[...], preferred_element_type=F32)
    o_ref[...] = x + gate * proj


def _ple(x2d, p2d, g, wg_bf, wp_bf, *, tm=512):
    t, d = x2d.shape
    q = p2d.shape[1]
    row = lambda i: (i, 0)
    return pl.pallas_call(
        _ple_kernel,
        grid=(t // tm,),
        in_specs=[pl.BlockSpec((tm, d), row), pl.BlockSpec((tm, q), row), _resident((1, d)),
                  _resident(wg_bf.shape), _resident(wp_bf.shape)],
        out_specs=pl.BlockSpec((tm, d), row),
        out_shape=jax.ShapeDtypeStruct((t, d), F32),
        compiler_params=_params(("arbitrary",)),
        name="ple",
    )(x2d, p2d, g.reshape(1, d), wg_bf, wp_bf)


def kernel(x, p, rel_bias, norm_mix, w_in, q_norm, k_norm, conv_w, conv_b, conv_ln_g, conv_ln_b,
           out_norm_attn, out_norm_conv, w_out, norm_ffn, w_router, w_gate, w_up, w_down,
           norm_ple, w_ple_gate, w_ple_proj):
    b, s, d = x.shape
    t = b * s
    ne = w_router.shape[-1]
    cap = EC_CAPACITY * s // ne
    x2d = x.reshape(t, d)
    for i in range(w_in.shape[0]):
        q, k, v, u = _in_proj(x2d, norm_mix[i], w_in[i].astype(BF16), q_norm[i], k_norm[i])
        a = q.shape[1]
        o_att = _attention(q.reshape(b, s, a), k.reshape(b, s, a), v.reshape(b, s, a), rel_bias)
        c_n = _conv_mixer(u.reshape(b, s, -1), conv_w[i], conv_b[i], conv_ln_g[i], conv_ln_b[i], out_norm_conv[i])
        x1, h2, aff = _out_proj(o_att.reshape(t, a), c_n.reshape(t, -1), x2d, out_norm_attn[i],
                                w_out[i].astype(BF16), norm_ffn[i], w_router[i])
        idx, gate = _route(aff, b, cap)
        x2 = _experts(idx, gate, h2, x1, w_gate[i], w_up[i], w_down[i], batch=b)
        x2d = _ple(x2, p[i].reshape(t, -1), norm_ple[i], w_ple_gate[i].astype(BF16), w_ple_proj[i].astype(BF16))
    return x2d.reshape(b, s, d)
```
